```python
import math
import jax, jax.numpy as jnp
from jax import lax
import numpy as np

D_MODEL = 2048
BATCH = 2
SEQ = 16384
DEPTH = 1

D_SSM = D_MODEL
SSM_HEAD_DIM = 64
N_SSM_HEADS = D_SSM // SSM_HEAD_DIM
SSM_GROUPS = 4
D_STATE = 128
CONV_WIDTH = 4
CONV_PAD = (2, 1)
CHUNK = 128
D_XBC = D_SSM + 2 * SSM_GROUPS * D_STATE

N_ATTN_HEADS = D_MODEL // 128
QK_NOPE_DIM = 128
QK_ROPE_DIM = 64
V_HEAD_DIM = 128
D_ATTN = N_ATTN_HEADS * V_HEAD_DIM
Q_LORA = 3 * D_MODEL // 8
KV_LORA = D_MODEL // 4
Q_BLOCK = 128
ROPE_THETA = 10000.0

EPS = 1e-6
D_IN = D_SSM + D_XBC + 2 * N_SSM_HEADS + Q_LORA + KV_LORA + QK_ROPE_DIM + D_ATTN + 2 * D_MODEL

kernel_name = "bidir_hybrid_ssd_mla_gated_merge"


def rmsnorm(x, w):
    xf = x.astype(jnp.float32)
    y = xf * lax.rsqrt(jnp.mean(xf * xf, axis=-1, keepdims=True) + EPS)
    return (y * w.astype(jnp.float32)).astype(x.dtype)


def segsum(a):
    t = a.shape[-1]
    ar = jnp.broadcast_to(a[..., :, None], a.shape + (t,))
    strict = jnp.tril(jnp.ones((t, t), dtype=bool), -1)
    cs = jnp.cumsum(jnp.where(strict, ar, 0.0), axis=-2)
    return jnp.where(jnp.tril(jnp.ones((t, t), dtype=bool)), cs, -jnp.inf)


def ssd_scan(xs, dt, a, bm, cm):
    b, s, nh, p = xs.shape
    g = bm.shape[2]
    j = nh // g
    c = s // CHUNK
    da = (dt * a).reshape(b, c, CHUNK, g, j).transpose(0, 3, 4, 1, 2)
    xd = (xs.astype(jnp.float32) * dt[..., None]).reshape(b, c, CHUNK, g, j, p)
    bc = bm.astype(jnp.float32).reshape(b, c, CHUNK, g, -1)
    cc = cm.astype(jnp.float32).reshape(b, c, CHUNK, g, -1)
    a_cs = jnp.cumsum(da, axis=-1)
    lmat = jnp.exp(segsum(da))
    cb = jnp.einsum("bclgn,bcsgn->bcgls", cc, bc)
    y_diag = jnp.einsum("bcgls,bgjcls,bcsgjp->bclgjp", cb, lmat, xd)
    decay_states = jnp.exp(a_cs[..., -1:] - a_cs)
    states = jnp.einsum("bcsgn,bgjcs,bcsgjp->bcgjpn", bc, decay_states, xd)
    a_last = jnp.pad(a_cs[..., -1], ((0, 0), (0, 0), (0, 0), (1, 0)))
    decay_chunk = jnp.exp(segsum(a_last))
    states = jnp.concatenate([jnp.zeros_like(states[:, :1]), states], axis=1)
    states = jnp.einsum("bgjzc,bcgjpn->bzgjpn", decay_chunk, states)[:, :-1]
    y_off = jnp.einsum("bclgn,bcgjpn,bgjcl->bclgjp", cc, states, jnp.exp(a_cs))
    return (y_diag + y_off).reshape(b, s, nh, p)


def dwconv(u, w, bias):
    out = lax.conv_general_dilated(
        u, w[:, None, :].astype(u.dtype), window_strides=(1,), padding=[CONV_PAD],
        dimension_numbers=("NWC", "WIO", "NWC"), feature_group_count=u.shape[-1])
    return out + bias.astype(u.dtype)


def rope_tables(positions):
    inv = 1.0 / (ROPE_THETA ** (jnp.arange(0, QK_ROPE_DIM, 2, dtype=jnp.float32) / QK_ROPE_DIM))
    ang = positions.astype(jnp.float32)[..., None] * inv
    return jnp.cos(ang), jnp.sin(ang)


def apply_rope(t, cos, sin):
    t1, t2 = jnp.split(t.astype(jnp.float32), 2, axis=-1)
    return jnp.concatenate([t1 * cos - t2 * sin, t2 * cos + t1 * sin], axis=-1).astype(t.dtype)


def bidir_attention(qn, qr, kn, kr, v):
    b, s, nh, _ = qn.shape
    nblk = s // Q_BLOCK
    scale = 1.0 / math.sqrt(QK_NOPE_DIM + QK_ROPE_DIM)

    def blocks(t):
        return t.reshape((b, nblk, Q_BLOCK) + t.shape[2:]).swapaxes(0, 1)

    def one(qs):
        qn_b, qr_b = qs
        sc = (jnp.einsum("bqhd,bkhd->bhqk", qn_b, kn).astype(jnp.float32)
              + jnp.einsum("bqhr,bkr->bhqk", qr_b, kr).astype(jnp.float32))
        p = jax.nn.softmax(sc * scale, axis=-1).astype(v.dtype)
        return jnp.einsum("bhqk,bkhd->bqhd", p, v)

    o = lax.map(one, (blocks(qn), blocks(qr)))
    return o.swapaxes(0, 1).reshape(b, s, nh * v.shape[-1])


def hybrid_layer(x, cos, sin, norm_w, w_in, conv_w, conv_b, a_log_fwd, a_log_bwd, dt_bias_fwd, dt_bias_bwd,
                 d_skip, ssm_norm_w, q_norm_w, w_uq, kv_norm_w, w_ukv, w_proj_ssm, w_proj_attn, w_out):
    b, s, _ = x.shape
    h = rmsnorm(x, norm_w)
    u = h @ w_in
    sizes = (D_SSM, D_XBC, N_SSM_HEADS, N_SSM_HEADS, Q_LORA, KV_LORA + QK_ROPE_DIM, D_ATTN, D_MODEL, D_MODEL)
    cuts = np.cumsum(sizes)[:-1].tolist()
    z_ssm, xbc, dt_f, dt_b, q_a, kv_a, z_attn, g_ssm, g_attn = jnp.split(u, cuts, axis=-1)

    xbc = jax.nn.silu(dwconv(xbc, conv_w, conv_b))
    xs, bm, cm = jnp.split(xbc, [D_SSM, D_SSM + SSM_GROUPS * D_STATE], axis=-1)
    xs = xs.reshape(b, s, N_SSM_HEADS, SSM_HEAD_DIM)
    bm = bm.reshape(b, s, SSM_GROUPS, D_STATE)
    cm = cm.reshape(b, s, SSM_GROUPS, D_STATE)
    dtf = jax.nn.softplus(dt_f.astype(jnp.float32) + dt_bias_fwd.astype(jnp.float32))
    dtb = jax.nn.softplus(dt_b.astype(jnp.float32) + dt_bias_bwd.astype(jnp.float32))
    a_f = -jnp.exp(a_log_fwd.astype(jnp.float32))
    a_b = -jnp.exp(a_log_bwd.astype(jnp.float32))
    flip = lambda t: jnp.flip(t, axis=1)
    y_fwd = ssd_scan(xs, dtf, a_f, bm, cm)
    y_bwd = flip(ssd_scan(flip(xs), flip(dtb), a_b, flip(bm), flip(cm)))
    y = (y_fwd + y_bwd + d_skip.astype(jnp.float32)[:, None] * xs.astype(jnp.float32)).astype(x.dtype)
    y = y.reshape(b, s, D_SSM) * jax.nn.silu(z_ssm)
    y = rmsnorm(y.reshape(b, s, SSM_GROUPS, D_SSM // SSM_GROUPS),
                ssm_norm_w.reshape(SSM_GROUPS, D_SSM // SSM_GROUPS)).reshape(b, s, D_SSM)

    c_q = rmsnorm(q_a, q_norm_w)
    q = (c_q @ w_uq).reshape(b, s, N_ATTN_HEADS, QK_NOPE_DIM + QK_ROPE_DIM)
    qn, qr = jnp.split(q, [QK_NOPE_DIM], axis=-1)
    c_kv, kr = jnp.split(kv_a, [KV_LORA], axis=-1)
    c_kv = rmsnorm(c_kv, kv_norm_w)
    kv = (c_kv @ w_ukv).reshape(b, s, N_ATTN_HEADS, QK_NOPE_DIM + V_HEAD_DIM)
    kn, v = jnp.split(kv, [QK_NOPE_DIM], axis=-1)
    qr = apply_rope(qr, cos[:, :, None, :], sin[:, :, None, :])
    kr = apply_rope(kr, cos, sin)
    o = bidir_attention(qn, qr, kn, kr, v) * jax.nn.silu(z_attn)

    merged = jax.nn.sigmoid(g_ssm) * (y @ w_proj_ssm) + jax.nn.sigmoid(g_attn) * (o @ w_proj_attn)
    return x + merged @ w_out


def setup_inputs(seed: int = 0) -> dict:
    key = jax.random.key(seed)
    ks = jax.random.split(key, 24)
    f32 = jnp.float32

    def nrm(k, shape, fan_in):
        return jax.random.normal(k, shape, f32) * (fan_in ** -0.5)

    def gain(k, shape):
        return 1.0 + 0.02 * jax.random.normal(k, shape, f32)

    def a_log(k):
        return jnp.log(jax.random.uniform(k, (DEPTH, N_SSM_HEADS), f32, minval=1.0, maxval=16.0))

    def dt_bias(k):
        lo, hi = math.log(0.001), math.log(0.1)
        dt = jnp.exp(jax.random.uniform(k, (DEPTH, N_SSM_HEADS), f32) * (hi - lo) + lo)
        return dt + jnp.log(-jnp.expm1(-dt))

    x = jax.random.normal(ks[0], (BATCH, SEQ, D_MODEL), f32)
    offsets = jax.random.randint(ks[1], (BATCH, 1), 0, 4096, dtype=jnp.int32)
    positions = (jnp.arange(SEQ, dtype=jnp.int32)[None, :] + offsets).astype(jnp.int32)
    return {
        "x": x,
        "positions": positions,
        "norm_w": gain(ks[2], (DEPTH, D_MODEL)),
        "w_in": nrm(ks[3], (DEPTH, D_MODEL, D_IN), D_MODEL),
        "conv_w": nrm(ks[4], (DEPTH, CONV_WIDTH, D_XBC), CONV_WIDTH),
        "conv_b": 0.02 * jax.random.normal(ks[5], (DEPTH, D_XBC), f32),
        "a_log_fwd": a_log(ks[6]),
        "a_log_bwd": a_log(ks[7]),
        "dt_bias_fwd": dt_bias(ks[8]),
        "dt_bias_bwd": dt_bias(ks[9]),
        "d_skip": gain(ks[10], (DEPTH, N_SSM_HEADS)),
        "ssm_norm_w": gain(ks[11], (DEPTH, D_SSM)),
        "q_norm_w": gain(ks[12], (DEPTH, Q_LORA)),
        "w_uq": nrm(ks[13], (DEPTH, Q_LORA, N_ATTN_HEADS * (QK_NOPE_DIM + QK_ROPE_DIM)), Q_LORA),
        "kv_norm_w": gain(ks[14], (DEPTH, KV_LORA)),
        "w_ukv": nrm(ks[15], (DEPTH, KV_LORA, N_ATTN_HEADS * (QK_NOPE_DIM + V_HEAD_DIM)), KV_LORA),
        "w_proj_ssm": nrm(ks[16], (DEPTH, D_SSM, D_MODEL), D_SSM),
        "w_proj_attn": nrm(ks[17], (DEPTH, D_ATTN, D_MODEL), D_ATTN),
        "w_out": nrm(ks[18], (DEPTH, D_MODEL, D_MODEL), D_MODEL),
        "final_norm_w": gain(ks[19], (D_MODEL,)),
    }


def reference(x, positions, norm_w, w_in, conv_w, conv_b, a_log_fwd, a_log_bwd, dt_bias_fwd, dt_bias_bwd,
              d_skip, ssm_norm_w, q_norm_w, w_uq, kv_norm_w, w_ukv, w_proj_ssm, w_proj_attn, w_out,
              final_norm_w):
    cos, sin = rope_tables(positions)
    for l in range(DEPTH):
        x = hybrid_layer(x, cos, sin, norm_w[l], w_in[l], conv_w[l], conv_b[l], a_log_fwd[l], a_log_bwd[l],
                         dt_bias_fwd[l], dt_bias_bwd[l], d_skip[l], ssm_norm_w[l], q_norm_w[l], w_uq[l],
                         kv_norm_w[l], w_ukv[l], w_proj_ssm[l], w_proj_attn[l], w_out[l])
    return rmsnorm(x, final_norm_w)
```

```python
import functools
import math

import jax
import jax.numpy as jnp
from jax import lax
from jax.experimental import pallas as pl
from jax.experimental.pallas import tpu as pltpu

F32 = jnp.float32
BF16 = jnp.bfloat16

D_MODEL = 2048
D_SSM = 2048
SSM_HEAD_DIM = 64
N_SSM_HEADS = 32
SSM_GROUPS = 4
HEADS_PER_GROUP = N_SSM_HEADS // SSM_GROUPS
D_STATE = 128
CHUNK = 128
D_XBC = D_SSM + 2 * SSM_GROUPS * D_STATE
N_ATTN_HEADS = 16
QK_NOPE_DIM = 128
QK_ROPE_DIM = 64
V_HEAD_DIM = 128
D_ATTN = N_ATTN_HEADS * V_HEAD_DIM
Q_LORA = 768
KV_LORA = 512
ROPE_THETA = 10000.0
EPS = 1e-6

LANES = 128
QK_PAD_DIM = 2 * LANES
QX_WIDTH = 1024
GATES_WIDTH = 4 * D_MODEL
W_IN_PERM_WIDTH = GATES_WIDTH + D_XBC + KV_LORA + QX_WIDTH

VMEM_LIMIT = 56 * 1024 * 1024


def _cparams(*sem):
    return pltpu.CompilerParams(dimension_semantics=sem, vmem_limit_bytes=VMEM_LIMIT)


def _sigmoid(x):
    return 1.0 / (1.0 + jnp.exp(-x))


def _silu(x):
    return x * _sigmoid(x)


def _softplus(x):
    return jnp.maximum(x, 0.0) + jnp.log1p(jnp.exp(-jnp.abs(x)))


def _rms(x, w):
    ms = jnp.mean(x * x, axis=-1, keepdims=True)
    return x * lax.rsqrt(ms + EPS) * w


def _rope_kernel(pos_ref, inv_ref, cos_ref, sina_ref, sinb_ref):
    ang = pos_ref[...].astype(F32) * inv_ref[...]
    lane = lax.broadcasted_iota(jnp.int32, ang.shape, 1)
    c = jnp.cos(ang)
    s = jnp.sin(ang)
    cos_ref[...] = jnp.where(lane < 64, c, 0.0)
    sina_ref[...] = jnp.where(lane < 32, -s, 0.0)
    sinb_ref[...] = jnp.where((lane >= 32) & (lane < 64), s, 0.0)


def _rope_tables(pos_col, inv_row, tm):
    n = pos_col.shape[0]
    tab = jax.ShapeDtypeStruct((n, LANES), F32)
    spec = pl.BlockSpec((tm, LANES), lambda i: (i, 0))
    return pl.pallas_call(
        _rope_kernel,
        grid=(n // tm,),
        in_specs=[pl.BlockSpec((tm, 1), lambda i: (i, 0)), pl.BlockSpec((1, LANES), lambda i: (0, 0))],
        out_specs=[spec, spec, spec],
        out_shape=[tab, tab, tab],
        compiler_params=_cparams("parallel"),
        name="rope_tables",
    )(pos_col, inv_row)


def _apply_rope(rp, cos, sina, sinb):
    return rp * cos + pltpu.roll(rp, 96, 1) * sina + pltpu.roll(rp, 32, 1) * sinb


IN_TN = 512
_J_GATES = GATES_WIDTH // IN_TN
_J_XBC = D_XBC // IN_TN
_J_CKV = KV_LORA // IN_TN
_J_QX = QX_WIDTH // IN_TN
_J0_XBC = _J_GATES
_J0_CKV = _J0_XBC + _J_XBC
_J0_QX = _J0_CKV + _J_CKV
_J_TOTAL = _J0_QX + _J_QX


def _inproj_kernel(x_ref, nw_ref, w_ref, gates_ref, xbc_ref, ckv_ref, qx_ref, dt_ref, h_ref):
    j = pl.program_id(1)

    @pl.when(j == 0)
    def _():
        h_ref[...] = _rms(x_ref[...], nw_ref[...]).astype(BF16)

    acc = jnp.dot(h_ref[...], w_ref[...], preferred_element_type=F32)

    @pl.when(j < _J0_XBC)
    def _():
        gates_ref[...] = acc.astype(BF16)

    @pl.when((j >= _J0_XBC) & (j < _J0_CKV))
    def _():
        xbc_ref[...] = acc.astype(BF16)

    @pl.when(j == _J0_CKV)
    def _():
        ckv_ref[...] = acc.astype(BF16)

    @pl.when(j >= _J0_QX)
    def _():
        qx_ref[...] = acc.astype(BF16)

    @pl.when(j == _J_TOTAL - 1)
    def _():
        dt_ref[...] = acc[:, IN_TN - LANES:]


def _in_projection(x2, norm_w, w_perm, tm):
    n = x2.shape[0]
    grid = (n // tm, _J_TOTAL)
    return pl.pallas_call(
        _inproj_kernel,
        grid=grid,
        in_specs=[
            pl.BlockSpec((tm, D_MODEL), lambda i, j: (i, 0)),
            pl.BlockSpec((1, D_MODEL), lambda i, j: (0, 0)),
            pl.BlockSpec((D_MODEL, IN_TN), lambda i, j: (0, j)),
        ],
        out_specs=[
            pl.BlockSpec((tm, IN_TN), lambda i, j: (i, jnp.minimum(j, _J_GATES - 1))),
            pl.BlockSpec((tm, IN_TN), lambda i, j: (i, jnp.clip(j - _J0_XBC, 0, _J_XBC - 1))),
            pl.BlockSpec((tm, IN_TN), lambda i, j: (i, 0)),
            pl.BlockSpec((tm, IN_TN), lambda i, j: (i, jnp.clip(j - _J0_QX, 0, _J_QX - 1))),
            pl.BlockSpec((tm, LANES), lambda i, j: (i, 0)),
        ],
        out_shape=[
            jax.ShapeDtypeStruct((n, GATES_WIDTH), BF16),
            jax.ShapeDtypeStruct((n, D_XBC), BF16),
            jax.ShapeDtypeStruct((n, KV_LORA), BF16),
            jax.ShapeDtypeStruct((n, QX_WIDTH), BF16),
            jax.ShapeDtypeStruct((n, LANES), F32),
        ],
        scratch_shapes=[pltpu.VMEM((tm, D_MODEL), BF16)],
        compiler_params=_cparams("parallel", "arbitrary"),
        name="in_projection",
    )(x2, norm_w, w_perm)


CONV_HALO = 16


def _conv_kernel(prev_ref, main_ref, next_ref, w_ref, b_ref, out_ref, ext_ref, *, tm):
    i = pl.program_id(1)
    last = pl.num_programs(1) - 1
    pm = jnp.where(i > 0, 1.0, 0.0).astype(F32)
    nm = jnp.where(i < last, 1.0, 0.0).astype(F32)
    ext_ref[0:CONV_HALO, :] = prev_ref[0].astype(F32) * pm
    ext_ref[CONV_HALO:CONV_HALO + tm, :] = main_ref[0].astype(F32)
    ext_ref[CONV_HALO + tm:2 * CONV_HALO + tm, :] = next_ref[0].astype(F32) * nm
    acc = b_ref[...] + w_ref[0:1, :] * ext_ref[CONV_HALO - 2:CONV_HALO - 2 + tm, :]
    acc = acc + w_ref[1:2, :] * ext_ref[CONV_HALO - 1:CONV_HALO - 1 + tm, :]
    acc = acc + w_ref[2:3, :] * ext_ref[CONV_HALO:CONV_HALO + tm, :]
    acc = acc + w_ref[3:4, :] * ext_ref[CONV_HALO + 1:CONV_HALO + 1 + tm, :]
    out_ref[0] = _silu(acc).astype(BF16)


def _conv_silu(xbc3, conv_w8, conv_b, tm, tc):
    b, s, c = xbc3.shape
    hb = tm // CONV_HALO
    n_halo = s // CONV_HALO
    return pl.pallas_call(
        functools.partial(_conv_kernel, tm=tm),
        grid=(b, s // tm, c // tc),
        in_specs=[
            pl.BlockSpec((1, CONV_HALO, tc), lambda bi, i, ci: (bi, jnp.maximum(i * hb - 1, 0), ci)),
            pl.BlockSpec((1, tm, tc), lambda bi, i, ci: (bi, i, ci)),
            pl.BlockSpec((1, CONV_HALO, tc), lambda bi, i, ci: (bi, jnp.minimum((i + 1) * hb, n_halo - 1), ci)),
            pl.BlockSpec((8, tc), lambda bi, i, ci: (0, ci)),
            pl.BlockSpec((1, tc), lambda bi, i, ci: (0, ci)),
        ],
        out_specs=pl.BlockSpec((1, tm, tc), lambda bi, i, ci: (bi, i, ci)),
        out_shape=jax.ShapeDtypeStruct((b, s, c), BF16),
        scratch_shapes=[pltpu.VMEM((tm + 2 * CONV_HALO, tc), F32)],
        compiler_params=_cparams("parallel", "parallel", "parallel"),
        name="conv_silu",
    )(xbc3, xbc3, xbc3, conv_w8, conv_b)


def _ssd_chunk(direction, xc_ref, dt_ref, bias_ref, alog_ref, state_ref, y_ref):
    fwd = direction == 0
    xc = xc_ref[...]
    dtv = _softplus(dt_ref[...] + bias_ref[...])
    da = dtv * (-jnp.exp(alog_ref[...]))
    row = lax.broadcasted_iota(jnp.int32, (CHUNK, CHUNK), 0)
    col = lax.broadcasted_iota(jnp.int32, (CHUNK, CHUNK), 1)
    keep = (row >= col) if fwd else (row <= col)
    a_cs = jnp.dot(keep.astype(F32), da, precision=lax.Precision.HIGHEST, preferred_element_type=F32)
    a_cs_t = a_cs.T
    dt_t = dtv.T
    end = CHUNK - 1 if fwd else 0
    lane = lax.broadcasted_iota(jnp.int32, (1, LANES), 1)
    first_half = lane < SSM_HEAD_DIM

    for g in range(SSM_GROUPS):
        b_g = xc[:, D_SSM + g * D_STATE:D_SSM + (g + 1) * D_STATE]
        c_g = xc[:, D_SSM + SSM_GROUPS * D_STATE + g * D_STATE:D_SSM + SSM_GROUPS * D_STATE + (g + 1) * D_STATE]
        cb = lax.dot_general(c_g, b_g, (((1,), (1,)), ((), ())), preferred_element_type=F32)
        c_f = c_g.astype(F32)
        b_t = b_g.astype(F32).T
        for pair in range(HEADS_PER_GROUP // 2):
            p_idx = g * (HEADS_PER_GROUP // 2) + pair
            x_pair = xc[:, p_idx * LANES:(p_idx + 1) * LANES]
            s_old = state_ref[:, p_idx * LANES:(p_idx + 1) * LANES]
            rhs = jnp.concatenate([x_pair, s_old.astype(BF16)], axis=0)
            ys, news, decs = [], [], []
            for k in range(2):
                cidx = direction * N_SSM_HEADS + 2 * p_idx + k
                col_a = a_cs[:, cidx:cidx + 1]
                row_a = a_cs_t[cidx:cidx + 1, :]
                row_dt = dt_t[cidx:cidx + 1, :]
                decay = jnp.where(keep, jnp.exp(col_a - row_a), 0.0)
                m_h = (cb * decay * row_dt).astype(BF16)
                c_e = (c_f * jnp.exp(col_a)).astype(BF16)
                lhs = jnp.concatenate([m_h, c_e], axis=1)
                ys.append(jnp.dot(lhs, rhs, preferred_element_type=F32))
                tot = row_a[:, end:end + 1]
                w_row = jnp.exp(tot - row_a) * row_dt
                b_w = (b_t * w_row).astype(BF16)
                news.append(jnp.dot(b_w, x_pair, preferred_element_type=F32))
                decs.append(jnp.exp(tot))
            y_ref[:, p_idx * LANES:(p_idx + 1) * LANES] = jnp.where(first_half, ys[0], ys[1])
            dec = jnp.where(first_half, decs[0], decs[1])
            state_ref[:, p_idx * LANES:(p_idx + 1) * LANES] = s_old * dec + jnp.where(first_half, news[0], news[1])


def _ssd_fwd_kernel(xc_ref, dt_ref, bias_ref, alog_ref, yf_ref, state_ref, y_ref):
    @pl.when(pl.program_id(1) == 0)
    def _():
        state_ref[...] = jnp.zeros_like(state_ref)

    _ssd_chunk(0, xc_ref, dt_ref, bias_ref, alog_ref, state_ref, y_ref)
    yf_ref[...] = y_ref[...].astype(BF16)


def _ssd_bwd_kernel(xc_ref, dt_ref, bias_ref, alog_ref, yf_ref, z_ref, dskip_ref, nw_ref, out_ref, state_ref, y_ref):
    @pl.when(pl.program_id(1) == 0)
    def _():
        state_ref[...] = jnp.zeros_like(state_ref)

    _ssd_chunk(1, xc_ref, dt_ref, bias_ref, alog_ref, state_ref, y_ref)
    gw = D_SSM // SSM_GROUPS
    for g in range(SSM_GROUPS):
        sl = slice(g * gw, (g + 1) * gw)
        xs = xc_ref[:, sl].astype(F32)
        y = yf_ref[:, sl].astype(F32) + y_ref[:, sl] + dskip_ref[:, sl] * xs
        y = y * _silu(z_ref[:, sl].astype(F32))
        out_ref[:, sl] = _rms(y, nw_ref[:, sl]).astype(BF16)


def _ssd(xc2, dt2, bias_row, alog_row, gates, d_skip_row, ssm_norm_w, batch, seq):
    n = xc2.shape[0]
    nc = seq // CHUNK
    row_spec = lambda width: pl.BlockSpec((CHUNK, width), lambda b, c: (b * nc + c, 0))
    rev_spec = lambda width: pl.BlockSpec((CHUNK, width), lambda b, c: (b * nc + nc - 1 - c, 0))
    const = lambda width: pl.BlockSpec((1, width), lambda b, c: (0, 0))
    scratch = [pltpu.VMEM((D_STATE, D_SSM), F32), pltpu.VMEM((CHUNK, D_SSM), F32)]
    y_fwd = pl.pallas_call(
        _ssd_fwd_kernel,
        grid=(batch, nc),
        in_specs=[row_spec(D_XBC), row_spec(LANES), const(LANES), const(LANES)],
        out_specs=row_spec(D_SSM),
        out_shape=jax.ShapeDtypeStruct((n, D_SSM), BF16),
        scratch_shapes=scratch,
        compiler_params=_cparams("parallel", "arbitrary"),
        name="ssd_forward",
    )(xc2, dt2, bias_row, alog_row)
    return pl.pallas_call(
        _ssd_bwd_kernel,
        grid=(batch, nc),
        in_specs=[rev_spec(D_XBC), rev_spec(LANES), const(LANES), const(LANES), rev_spec(D_SSM),
                  rev_spec(D_SSM), const(D_SSM), const(D_SSM)],
        out_specs=rev_spec(D_SSM),
        out_shape=jax.ShapeDtypeStruct((n, D_SSM), BF16),
        scratch_shapes=scratch,
        compiler_params=_cparams("parallel", "arbitrary"),
        name="ssd_backward",
    )(xc2, dt2, bias_row, alog_row, y_fwd, gates, d_skip_row, ssm_norm_w)


Q_SCALE = (1.0 / math.sqrt(QK_NOPE_DIM + QK_ROPE_DIM)) * math.log2(math.e)


def _qproj_kernel(qx_ref, nw_ref, w_ref, cos_ref, sina_ref, sinb_ref, q_ref):
    cq = _rms(qx_ref[:, :Q_LORA].astype(F32), nw_ref[...]).astype(BF16)
    cos, sina, sinb = cos_ref[...], sina_ref[...], sinb_ref[...]
    for h in range(N_ATTN_HEADS):
        a = jnp.dot(cq, w_ref[:, h * QK_PAD_DIM:(h + 1) * QK_PAD_DIM], preferred_element_type=F32)
        q_ref[0, h, :, :LANES] = (a[:, :LANES] * Q_SCALE).astype(BF16)
        q_ref[0, h, :, LANES:] = (_apply_rope(a[:, LANES:], cos, sina, sinb) * Q_SCALE).astype(BF16)


def _q_projection(qx, q_norm_w, w_q, cos, sina, sinb, batch, seq, tm):
    nq = seq // tm
    tab = pl.BlockSpec((tm, LANES), lambda i: (i, 0))
    return pl.pallas_call(
        _qproj_kernel,
        grid=(batch * nq,),
        in_specs=[
            pl.BlockSpec((tm, QX_WIDTH), lambda i: (i, 0)),
            pl.BlockSpec((1, Q_LORA), lambda i: (0, 0)),
            pl.BlockSpec((Q_LORA, N_ATTN_HEADS * QK_PAD_DIM), lambda i: (0, 0)),
            tab, tab, tab,
        ],
        out_specs=pl.BlockSpec((1, N_ATTN_HEADS, tm, QK_PAD_DIM), lambda i: (i // nq, 0, i % nq, 0)),
        out_shape=jax.ShapeDtypeStruct((batch, N_ATTN_HEADS, seq, QK_PAD_DIM), BF16),
        compiler_params=_cparams("parallel"),
        name="q_projection",
    )(qx, q_norm_w, w_q, cos, sina, sinb)


ATTN_TK = 512


def _kvproj_kernel(ckv_ref, kr_ref, nw_ref, wkt_ref, wv_ref, cos_ref, sina_ref, sinb_ref, kt_ref, v_ref):
    c = _rms(ckv_ref[...].astype(F32), nw_ref[...]).astype(BF16)
    kr = _apply_rope(kr_ref[...].astype(F32), cos_ref[...], sina_ref[...], sinb_ref[...])
    kr_t = kr.T.astype(BF16)
    for h in range(N_ATTN_HEADS):
        kn_t = lax.dot_general(wkt_ref[h * LANES:(h + 1) * LANES, :], c, (((1,), (1,)), ((), ())),
                               preferred_element_type=F32)
        kt_ref[0, h, 0, :LANES, :] = kn_t.astype(BF16)
        kt_ref[0, h, 0, LANES:, :] = kr_t
        v = jnp.dot(c, wv_ref[:, h * V_HEAD_DIM:(h + 1) * V_HEAD_DIM], preferred_element_type=F32)
        v_ref[0, h] = v.astype(BF16)


def _kv_projection(ckv, qx, kv_norm_w, w_kt, w_v, cos, sina, sinb, batch, seq):
    tk = ATTN_TK
    nk = seq // tk
    tab = pl.BlockSpec((tk, LANES), lambda i: (i, 0))
    return pl.pallas_call(
        _kvproj_kernel,
        grid=(batch * nk,),
        in_specs=[
            pl.BlockSpec((tk, KV_LORA), lambda i: (i, 0)),
            pl.BlockSpec((tk, LANES), lambda i: (i, Q_LORA // LANES)),
            pl.BlockSpec((1, KV_LORA), lambda i: (0, 0)),
            pl.BlockSpec((N_ATTN_HEADS * QK_NOPE_DIM, KV_LORA), lambda i: (0, 0)),
            pl.BlockSpec((KV_LORA, N_ATTN_HEADS * V_HEAD_DIM), lambda i: (0, 0)),
            tab, tab, tab,
        ],
        out_specs=[
            pl.BlockSpec((1, N_ATTN_HEADS, 1, QK_PAD_DIM, tk), lambda i: (i // nk, 0, i % nk, 0, 0)),
            pl.BlockSpec((1, N_ATTN_HEADS, tk, V_HEAD_DIM), lambda i: (i // nk, 0, i % nk, 0)),
        ],
        out_shape=[
            jax.ShapeDtypeStruct((batch, N_ATTN_HEADS, nk, QK_PAD_DIM, tk), BF16),
            jax.ShapeDtypeStruct((batch, N_ATTN_HEADS, seq, V_HEAD_DIM), BF16),
        ],
        compiler_params=_cparams("parallel"),
        name="kv_projection",
    )(ckv, qx, kv_norm_w, w_kt, w_v, cos, sina, sinb)


def _attn_kernel(q_ref, kt_ref, v_ref, z_ref, o_ref, m_ref, l_ref, acc_ref, *, tk, nk):
    q = q_ref[0, 0]
    m_ref[...] = jnp.full_like(m_ref, -jnp.inf)
    l_ref[...] = jnp.zeros_like(l_ref)
    acc_ref[...] = jnp.zeros_like(acc_ref)
    reps = tk // LANES

    def body(j, carry):
        s = jnp.dot(q, kt_ref[0, 0, j], preferred_element_type=F32)
        m_prev = m_ref[...]
        m_new = jnp.maximum(m_prev, jnp.max(s, axis=1, keepdims=True))
        alpha = jnp.exp2(m_prev - m_new)
        p = jnp.exp2(s - jnp.concatenate([m_new] * reps, axis=1))
        l_part = p[:, :LANES]
        for r in range(1, reps):
            l_part = l_part + p[:, r * LANES:(r + 1) * LANES]
        l_ref[...] = alpha * l_ref[...] + l_part
        start = pl.multiple_of(j * tk, tk)
        pv = jnp.dot(p.astype(BF16), v_ref[0, 0, pl.ds(start, tk), :], preferred_element_type=F32)
        acc_ref[...] = alpha * acc_ref[...] + pv
        m_ref[...] = m_new
        return carry

    lax.fori_loop(0, nk, body, 0)
    l = jnp.sum(l_ref[...], axis=1, keepdims=True)
    o_ref[...] = (acc_ref[...] / l * _silu(z_ref[...].astype(F32))).astype(BF16)


def _attention(q, kt, v, gates, batch, seq, tq):
    tk = ATTN_TK
    nk = seq // tk
    nq = seq // tq
    z_col0 = D_MODEL // V_HEAD_DIM
    return pl.pallas_call(
        functools.partial(_attn_kernel, tk=tk, nk=nk),
        grid=(batch, N_ATTN_HEADS, nq),
        in_specs=[
            pl.BlockSpec((1, 1, tq, QK_PAD_DIM), lambda b, h, i: (b, h, i, 0)),
            pl.BlockSpec((1, 1, nk, QK_PAD_DIM, tk), lambda b, h, i: (b, h, 0, 0, 0)),
            pl.BlockSpec((1, 1, seq, V_HEAD_DIM), lambda b, h, i: (b, h, 0, 0)),
            pl.BlockSpec((tq, V_HEAD_DIM), lambda b, h, i: (b * nq + i, z_col0 + h)),
        ],
        out_specs=pl.BlockSpec((tq, V_HEAD_DIM), lambda b, h, i: (b * nq + i, h)),
        out_shape=jax.ShapeDtypeStruct((batch * seq, D_ATTN), BF16),
        scratch_shapes=[pltpu.VMEM((tq, LANES), F32), pltpu.VMEM((tq, LANES), F32),
                        pltpu.VMEM((tq, V_HEAD_DIM), F32)],
        compiler_params=_cparams("parallel", "parallel", "arbitrary"),
        name="attention",
    )(q, kt, v, gates)


def _merge_kernel(y_ref, o_ref, gs_ref, ga_ref, wps_ref, wpa_ref, out_ref):
    a = jnp.dot(y_ref[...], wps_ref[...], preferred_element_type=F32)
    merged = _sigmoid(gs_ref[...].astype(F32)) * a
    b = jnp.dot(o_ref[...], wpa_ref[...], preferred_element_type=F32)
    merged = merged + _sigmoid(ga_ref[...].astype(F32)) * b
    out_ref[...] = merged.astype(BF16)


def _merge(y, o, gates, w_ps, w_pa, tm):
    n = y.shape[0]
    row = pl.BlockSpec((tm, D_MODEL), lambda i: (i, 0))
    wspec = pl.BlockSpec((D_MODEL, D_MODEL), lambda i: (0, 0))
    return pl.pallas_call(
        _merge_kernel,
        grid=(n // tm,),
        in_specs=[row, row,
                  pl.BlockSpec((tm, D_MODEL), lambda i: (i, 2)),
                  pl.BlockSpec((tm, D_MODEL), lambda i: (i, 3)),
                  wspec, wspec],
        out_specs=row,
        out_shape=jax.ShapeDtypeStruct((n, D_MODEL), BF16),
        compiler_params=_cparams("parallel"),
        name="gated_merge",
    )(y, o, gates, gates, w_ps, w_pa)


def _out_kernel(m_ref, x_ref, w_ref, nw_ref, out_ref):
    r = x_ref[...] + jnp.dot(m_ref[...], w_ref[...], preferred_element_type=F32)
    out_ref[...] = _rms(r, nw_ref[...])


def _out_projection(merged, x2, w_out, final_w, tm):
    n = x2.shape[0]
    row = pl.BlockSpec((tm, D_MODEL), lambda i: (i, 0))
    return pl.pallas_call(
        _out_kernel,
        grid=(n // tm,),
        in_specs=[row, row, pl.BlockSpec((D_MODEL, D_MODEL), lambda i: (0, 0)),
                  pl.BlockSpec((1, D_MODEL), lambda i: (0, 0))],
        out_specs=row,
        out_shape=jax.ShapeDtypeStruct((n, D_MODEL), F32),
        compiler_params=_cparams("parallel"),
        name="out_projection",
    )(merged, x2, w_out, final_w)


def _permute_w_in(w_in):
    sizes = (D_SSM, D_XBC, N_SSM_HEADS, N_SSM_HEADS, Q_LORA, KV_LORA, QK_ROPE_DIM, D_ATTN, D_MODEL, D_MODEL)
    offs = [0]
    for sz in sizes:
        offs.append(offs[-1] + sz)
    z_ssm, xbc, dt_f, dt_b, q_a, c_kv, k_r, z_attn, g_ssm, g_attn = (
        w_in[:, offs[k]:offs[k + 1]] for k in range(len(sizes)))
    pad64 = jnp.zeros((D_MODEL, LANES - QK_ROPE_DIM), w_in.dtype)
    cols = [z_ssm, z_attn, g_ssm, g_attn, xbc, c_kv, q_a, k_r, pad64, dt_f, dt_b, pad64]
    return jnp.concatenate(cols, axis=1).astype(BF16)


def _permute_w_uq(w_uq):
    w = w_uq.reshape(Q_LORA, N_ATTN_HEADS, QK_NOPE_DIM + QK_ROPE_DIM)
    pad = jnp.zeros((Q_LORA, N_ATTN_HEADS, QK_PAD_DIM - QK_NOPE_DIM - QK_ROPE_DIM), w.dtype)
    return jnp.concatenate([w, pad], axis=2).reshape(Q_LORA, N_ATTN_HEADS * QK_PAD_DIM).astype(BF16)


def _layer(x2, tabs, batch, seq, norm_w, w_in, conv_w, conv_b, a_log_fwd, a_log_bwd, dt_bias_fwd, dt_bias_bwd,
           d_skip, ssm_norm_w, q_norm_w, w_uq, kv_norm_w, w_ukv, w_proj_ssm, w_proj_attn, w_out):
    cos, sina, sinb = tabs
    row = lambda v: v.reshape(1, -1).astype(F32)
    pad_row = lambda f, b: jnp.concatenate([f, b, jnp.zeros((LANES - 2 * N_SSM_HEADS,), F32)]).reshape(1, LANES)

    gates, xbc, ckv, qx, dt = _in_projection(x2, row(norm_w), _permute_w_in(w_in), tm=1024)

    conv_w8 = jnp.concatenate([conv_w.astype(F32), jnp.zeros((4, D_XBC), F32)], axis=0)
    xc = _conv_silu(xbc.reshape(batch, seq, D_XBC), conv_w8, row(conv_b), tm=512, tc=512)
    y = _ssd(xc.reshape(batch * seq, D_XBC), dt, pad_row(dt_bias_fwd, dt_bias_bwd), pad_row(a_log_fwd, a_log_bwd),
             gates, row(jnp.repeat(d_skip, SSM_HEAD_DIM)), row(ssm_norm_w), batch, seq)

    w_kv = w_ukv.reshape(KV_LORA, N_ATTN_HEADS, QK_NOPE_DIM + V_HEAD_DIM)
    w_kt = w_kv[:, :, :QK_NOPE_DIM].reshape(KV_LORA, N_ATTN_HEADS * QK_NOPE_DIM).T.astype(BF16)
    w_v = w_kv[:, :, QK_NOPE_DIM:].reshape(KV_LORA, N_ATTN_HEADS * V_HEAD_DIM).astype(BF16)
    q = _q_projection(qx, row(q_norm_w), _permute_w_uq(w_uq), cos, sina, sinb, batch, seq, tm=512)
    kt, v = _kv_projection(ckv, qx, row(kv_norm_w), w_kt, w_v, cos, sina, sinb, batch, seq)
    o = _attention(q, kt, v, gates, batch, seq, tq=512)

    merged = _merge(y, o, gates, w_proj_ssm.astype(BF16), w_proj_attn.astype(BF16), tm=256)
    return merged, w_out.astype(BF16)


def kernel(x, positions, norm_w, w_in, conv_w, conv_b, a_log_fwd, a_log_bwd, dt_bias_fwd, dt_bias_bwd, d_skip,
           ssm_norm_w, q_norm_w, w_uq, kv_norm_w, w_ukv, w_proj_ssm, w_proj_attn, w_out, final_norm_w):
    batch, seq, _ = x.shape
    depth = norm_w.shape[0]
    assert depth == 1, "the single output projection is fused with the final norm"
    inv = 1.0 / (ROPE_THETA ** (jnp.arange(0, QK_ROPE_DIM, 2, dtype=F32) / QK_ROPE_DIM))
    inv_row = jnp.concatenate([inv, inv, jnp.zeros((LANES - QK_ROPE_DIM,), F32)]).reshape(1, LANES)
    tabs = _rope_tables(positions.reshape(batch * seq, 1), inv_row, tm=1024)
    x2 = x.reshape(batch * seq, D_MODEL)
    merged, w_o = _layer(x2, tabs, batch, seq, norm_w[0], w_in[0], conv_w[0], conv_b[0], a_log_fwd[0], a_log_bwd[0],
                         dt_bias_fwd[0], dt_bias_bwd[0], d_skip[0], ssm_norm_w[0], q_norm_w[0], w_uq[0],
                         kv_norm_w[0], w_ukv[0], w_proj_ssm[0], w_proj_attn[0], w_out[0])
    out = _out_projection(merged, x2, w_o, final_norm_w.reshape(1, -1).astype(F32), tm=512)
    return out.reshape(batch, seq, D_MODEL)
```

```python
import functools
import math

import jax
import jax.numpy as jnp
from jax import lax
from jax.experimental import pallas as pl
from jax.experimental.pallas import tpu as pltpu

F32 = jnp.float32
BF16 = jnp.bfloat16

D_MODEL = 2048
D_SSM = 2048
SSM_HEAD_DIM = 64
N_SSM_HEADS = 32
SSM_GROUPS = 4
HEADS_PER_GROUP = N_SSM_HEADS // SSM_GROUPS
D_STATE = 128
CHUNK = 128
D_XBC = D_SSM + 2 * SSM_GROUPS * D_STATE
N_ATTN_HEADS = 16
QK_NOPE_DIM = 128
QK_ROPE_DIM = 64
V_HEAD_DIM = 128
D_ATTN = N_ATTN_HEADS * V_HEAD_DIM
Q_LORA = 768
KV_LORA = 512
ROPE_THETA = 10000.0
EPS = 1e-6

LANES = 128
QK_PAD_DIM = 2 * LANES
QX_WIDTH = 1024
GATES_WIDTH = 4 * D_MODEL
W_IN_PERM_WIDTH = GATES_WIDTH + D_XBC + KV_LORA + QX_WIDTH

VMEM_LIMIT = 56 * 1024 * 1024


def _cparams(*sem):
    return pltpu.CompilerParams(dimension_semantics=sem, vmem_limit_bytes=VMEM_LIMIT)


def _sigmoid(x):
    return 1.0 / (1.0 + jnp.exp(-x))


def _silu(x):
    return x * _sigmoid(x)


def _softplus(x):
    return jnp.maximum(x, 0.0) + jnp.log1p(jnp.exp(-jnp.abs(x)))


def _rms(x, w):
    ms = jnp.mean(x * x, axis=-1, keepdims=True)
    return x * lax.rsqrt(ms + EPS) * w


def _rope_kernel(pos_ref, inv_ref, cos_ref, sina_ref, sinb_ref):
    ang = pos_ref[...].astype(F32) * inv_ref[...]
    lane = lax.broadcasted_iota(jnp.int32, ang.shape, 1)
    c = jnp.cos(ang)
    s = jnp.sin(ang)
    cos_ref[...] = jnp.where(lane < 64, c, 0.0)
    sina_ref[...] = jnp.where(lane < 32, -s, 0.0)
    sinb_ref[...] = jnp.where((lane >= 32) & (lane < 64), s, 0.0)


def _rope_tables(pos_col, inv_row, tm):
    n = pos_col.shape[0]
    tab = jax.ShapeDtypeStruct((n, LANES), F32)
    spec = pl.BlockSpec((tm, LANES), lambda i: (i, 0))
    return pl.pallas_call(
        _rope_kernel,
        grid=(n // tm,),
        in_specs=[pl.BlockSpec((tm, 1), lambda i: (i, 0)), pl.BlockSpec((1, LANES), lambda i: (0, 0))],
        out_specs=[spec, spec, spec],
        out_shape=[tab, tab, tab],
        compiler_params=_cparams("parallel"),
        name="rope_tables",
    )(pos_col, inv_row)


def _apply_rope(rp, cos, sina, sinb):
    return rp * cos + pltpu.roll(rp, 96, 1) * sina + pltpu.roll(rp, 32, 1) * sinb


IN_TN = 512
U_GATES = 0
U_QX = U_GATES + GATES_WIDTH
U_XBC = U_QX + QX_WIDTH
U_CKV = U_XBC + D_XBC
U_WIDTH = U_CKV + KV_LORA
_J_DT = (U_QX + QX_WIDTH - LANES) // IN_TN
assert (U_QX + QX_WIDTH) % IN_TN == 0 and U_WIDTH == W_IN_PERM_WIDTH


def _inproj_kernel(x_ref, nw_ref, w_ref, u_ref, dt_ref, h_ref):
    j = pl.program_id(1)

    @pl.when(j == 0)
    def _():
        h_ref[...] = _rms(x_ref[...], nw_ref[...]).astype(BF16)

    acc = jnp.dot(h_ref[...], w_ref[...], preferred_element_type=F32)
    u_ref[...] = acc.astype(BF16)

    @pl.when(j == _J_DT)
    def _():
        dt_ref[...] = acc[:, IN_TN - LANES:]


def _in_projection(x2, norm_w, w_perm, tm):
    n = x2.shape[0]
    grid = (n // tm, U_WIDTH // IN_TN)
    return pl.pallas_call(
        _inproj_kernel,
        grid=grid,
        in_specs=[
            pl.BlockSpec((tm, D_MODEL), lambda i, j: (i, 0)),
            pl.BlockSpec((1, D_MODEL), lambda i, j: (0, 0)),
            pl.BlockSpec((D_MODEL, IN_TN), lambda i, j: (0, j)),
        ],
        out_specs=[
            pl.BlockSpec((tm, IN_TN), lambda i, j: (i, j)),
            pl.BlockSpec((tm, LANES), lambda i, j: (i, 0)),
        ],
        out_shape=[
            jax.ShapeDtypeStruct((n, U_WIDTH), BF16),
            jax.ShapeDtypeStruct((n, LANES), F32),
        ],
        scratch_shapes=[pltpu.VMEM((tm, D_MODEL), BF16)],
        compiler_params=_cparams("parallel", "arbitrary"),
        name="in_projection",
    )(x2, norm_w, w_perm)


CONV_HALO = 16


def _conv_kernel(prev_ref, main_ref, next_ref, w_ref, b_ref, out_ref, ext_ref, *, tm):
    i = pl.program_id(1)
    last = pl.num_programs(1) - 1
    pm = jnp.where(i > 0, 1.0, 0.0).astype(F32)
    nm = jnp.where(i < last, 1.0, 0.0).astype(F32)
    ext_ref[0:CONV_HALO, :] = prev_ref[0].astype(F32) * pm
    ext_ref[CONV_HALO:CONV_HALO + tm, :] = main_ref[0].astype(F32)
    ext_ref[CONV_HALO + tm:2 * CONV_HALO + tm, :] = next_ref[0].astype(F32) * nm
    acc = b_ref[...] + w_ref[0:1, :] * ext_ref[CONV_HALO - 2:CONV_HALO - 2 + tm, :]
    acc = acc + w_ref[1:2, :] * ext_ref[CONV_HALO - 1:CONV_HALO - 1 + tm, :]
    acc = acc + w_ref[2:3, :] * ext_ref[CONV_HALO:CONV_HALO + tm, :]
    acc = acc + w_ref[3:4, :] * ext_ref[CONV_HALO + 1:CONV_HALO + 1 + tm, :]
    out_ref[0] = _silu(acc).astype(BF16)


def _conv_silu(u3, conv_w8, conv_b, tm, tc):
    b, s, _ = u3.shape
    hb = tm // CONV_HALO
    n_halo = s // CONV_HALO
    c0 = U_XBC // tc
    return pl.pallas_call(
        functools.partial(_conv_kernel, tm=tm),
        grid=(b, s // tm, D_XBC // tc),
        in_specs=[
            pl.BlockSpec((1, CONV_HALO, tc), lambda bi, i, ci: (bi, jnp.maximum(i * hb - 1, 0), c0 + ci)),
            pl.BlockSpec((1, tm, tc), lambda bi, i, ci: (bi, i, c0 + ci)),
            pl.BlockSpec((1, CONV_HALO, tc), lambda bi, i, ci: (bi, jnp.minimum((i + 1) * hb, n_halo - 1), c0 + ci)),
            pl.BlockSpec((8, tc), lambda bi, i, ci: (0, ci)),
            pl.BlockSpec((1, tc), lambda bi, i, ci: (0, ci)),
        ],
        out_specs=pl.BlockSpec((1, tm, tc), lambda bi, i, ci: (bi, i, ci)),
        out_shape=jax.ShapeDtypeStruct((b, s, D_XBC), BF16),
        scratch_shapes=[pltpu.VMEM((tm + 2 * CONV_HALO, tc), F32)],
        compiler_params=_cparams("parallel", "parallel", "parallel"),
        name="conv_silu",
    )(u3, u3, u3, conv_w8, conv_b)


def _ssd_chunk(direction, xc_ref, dt_ref, bias_ref, alog_ref, state_ref, y_ref):
    fwd = direction == 0
    xc = xc_ref[...]
    dtv = _softplus(dt_ref[...] + bias_ref[...])
    da = dtv * (-jnp.exp(alog_ref[...]))
    row = lax.broadcasted_iota(jnp.int32, (CHUNK, CHUNK), 0)
    col = lax.broadcasted_iota(jnp.int32, (CHUNK, CHUNK), 1)
    keep = (row >= col) if fwd else (row <= col)
    a_cs = jnp.dot(keep.astype(F32), da, precision=lax.Precision.HIGHEST, preferred_element_type=F32)
    a_cs_t = a_cs.T
    dt_t = dtv.T
    end = CHUNK - 1 if fwd else 0
    lane = lax.broadcasted_iota(jnp.int32, (1, LANES), 1)
    first_half = lane < SSM_HEAD_DIM

    for g in range(SSM_GROUPS):
        b_g = xc[:, D_SSM + g * D_STATE:D_SSM + (g + 1) * D_STATE]
        c_g = xc[:, D_SSM + SSM_GROUPS * D_STATE + g * D_STATE:D_SSM + SSM_GROUPS * D_STATE + (g + 1) * D_STATE]
        cb = lax.dot_general(c_g, b_g, (((1,), (1,)), ((), ())), preferred_element_type=F32)
        c_f = c_g.astype(F32)
        b_t = b_g.astype(F32).T
        for pair in range(HEADS_PER_GROUP // 2):
            p_idx = g * (HEADS_PER_GROUP // 2) + pair
            x_pair = xc[:, p_idx * LANES:(p_idx + 1) * LANES]
            s_old = state_ref[:, p_idx * LANES:(p_idx + 1) * LANES]
            rhs = jnp.concatenate([x_pair, s_old.astype(BF16)], axis=0)
            ys, news, decs = [], [], []
            for k in range(2):
                cidx = direction * N_SSM_HEADS + 2 * p_idx + k
                col_a = a_cs[:, cidx:cidx + 1]
                row_a = a_cs_t[cidx:cidx + 1, :]
                row_dt = dt_t[cidx:cidx + 1, :]
                decay = jnp.where(keep, jnp.exp(col_a - row_a), 0.0)
                m_h = (cb * decay * row_dt).astype(BF16)
                c_e = (c_f * jnp.exp(col_a)).astype(BF16)
                lhs = jnp.concatenate([m_h, c_e], axis=1)
                ys.append(jnp.dot(lhs, rhs, preferred_element_type=F32))
                tot = row_a[:, end:end + 1]
                w_row = jnp.exp(tot - row_a) * row_dt
                b_w = (b_t * w_row).astype(BF16)
                news.append(jnp.dot(b_w, x_pair, preferred_element_type=F32))
                decs.append(jnp.exp(tot))
            y_ref[:, p_idx * LANES:(p_idx + 1) * LANES] = jnp.where(first_half, ys[0], ys[1])
            dec = jnp.where(first_half, decs[0], decs[1])
            state_ref[:, p_idx * LANES:(p_idx + 1) * LANES] = s_old * dec + jnp.where(first_half, news[0], news[1])


def _ssd_fwd_kernel(xc_ref, dt_ref, bias_ref, alog_ref, yf_ref, state_ref, y_ref):
    @pl.when(pl.program_id(1) == 0)
    def _():
        state_ref[...] = jnp.zeros_like(state_ref)

    _ssd_chunk(0, xc_ref, dt_ref, bias_ref, alog_ref, state_ref, y_ref)
    yf_ref[...] = y_ref[...].astype(BF16)


def _ssd_bwd_kernel(xc_ref, dt_ref, bias_ref, alog_ref, yf_ref, z_ref, dskip_ref, nw_ref, out_ref, state_ref, y_ref):
    @pl.when(pl.program_id(1) == 0)
    def _():
        state_ref[...] = jnp.zeros_like(state_ref)

    _ssd_chunk(1, xc_ref, dt_ref, bias_ref, alog_ref, state_ref, y_ref)
    gw = D_SSM // SSM_GROUPS
    for g in range(SSM_GROUPS):
        sl = slice(g * gw, (g + 1) * gw)
        xs = xc_ref[:, sl].astype(F32)
        y = yf_ref[:, sl].astype(F32) + y_ref[:, sl] + dskip_ref[:, sl] * xs
        y = y * _silu(z_ref[:, sl].astype(F32))
        out_ref[:, sl] = _rms(y, nw_ref[:, sl]).astype(BF16)


def _ssd(xc2, dt2, bias_row, alog_row, gates, d_skip_row, ssm_norm_w, batch, seq):
    n = xc2.shape[0]
    nc = seq // CHUNK
    row_spec = lambda width: pl.BlockSpec((CHUNK, width), lambda b, c: (b * nc + c, 0))
    rev_spec = lambda width: pl.BlockSpec((CHUNK, width), lambda b, c: (b * nc + nc - 1 - c, 0))
    const = lambda width: pl.BlockSpec((1, width), lambda b, c: (0, 0))
    scratch = [pltpu.VMEM((D_STATE, D_SSM), F32), pltpu.VMEM((CHUNK, D_SSM), F32)]
    y_fwd = pl.pallas_call(
        _ssd_fwd_kernel,
        grid=(batch, nc),
        in_specs=[row_spec(D_XBC), row_spec(LANES), const(LANES), const(LANES)],
        out_specs=row_spec(D_SSM),
        out_shape=jax.ShapeDtypeStruct((n, D_SSM), BF16),
        scratch_shapes=scratch,
        compiler_params=_cparams("parallel", "arbitrary"),
        name="ssd_forward",
    )(xc2, dt2, bias_row, alog_row)
    return pl.pallas_call(
        _ssd_bwd_kernel,
        grid=(batch, nc),
        in_specs=[rev_spec(D_XBC), rev_spec(LANES), const(LANES), const(LANES), rev_spec(D_SSM),
                  rev_spec(D_SSM), const(D_SSM), const(D_SSM)],
        out_specs=rev_spec(D_SSM),
        out_shape=jax.ShapeDtypeStruct((n, D_SSM), BF16),
        scratch_shapes=scratch,
        compiler_params=_cparams("parallel", "arbitrary"),
        name="ssd_backward",
    )(xc2, dt2, bias_row, alog_row, y_fwd, gates, d_skip_row, ssm_norm_w)


Q_SCALE = (1.0 / math.sqrt(QK_NOPE_DIM + QK_ROPE_DIM)) * math.log2(math.e)


def _qproj_kernel(qx_ref, nw_ref, w_ref, cos_ref, sina_ref, sinb_ref, q_ref):
    cq = _rms(qx_ref[:, :Q_LORA].astype(F32), nw_ref[...]).astype(BF16)
    cos, sina, sinb = cos_ref[...], sina_ref[...], sinb_ref[...]
    for h in range(N_ATTN_HEADS):
        a = jnp.dot(cq, w_ref[:, h * QK_PAD_DIM:(h + 1) * QK_PAD_DIM], preferred_element_type=F32)
        q_ref[0, h, :, :LANES] = (a[:, :LANES] * Q_SCALE).astype(BF16)
        q_ref[0, h, :, LANES:] = (_apply_rope(a[:, LANES:], cos, sina, sinb) * Q_SCALE).astype(BF16)


def _q_projection(qx, q_norm_w, w_q, cos, sina, sinb, batch, seq, tm):
    nq = seq // tm
    tab = pl.BlockSpec((tm, LANES), lambda i: (i, 0))
    return pl.pallas_call(
        _qproj_kernel,
        grid=(batch * nq,),
        in_specs=[
            pl.BlockSpec((tm, QX_WIDTH), lambda i: (i, U_QX // QX_WIDTH)),
            pl.BlockSpec((1, Q_LORA), lambda i: (0, 0)),
            pl.BlockSpec((Q_LORA, N_ATTN_HEADS * QK_PAD_DIM), lambda i: (0, 0)),
            tab, tab, tab,
        ],
        out_specs=pl.BlockSpec((1, N_ATTN_HEADS, tm, QK_PAD_DIM), lambda i: (i // nq, 0, i % nq, 0)),
        out_shape=jax.ShapeDtypeStruct((batch, N_ATTN_HEADS, seq, QK_PAD_DIM), BF16),
        compiler_params=_cparams("parallel"),
        name="q_projection",
    )(qx, q_norm_w, w_q, cos, sina, sinb)


ATTN_TK = 512


def _kvproj_kernel(ckv_ref, kr_ref, nw_ref, wkt_ref, wv_ref, cos_ref, sina_ref, sinb_ref, kt_ref, v_ref):
    c = _rms(ckv_ref[...].astype(F32), nw_ref[...]).astype(BF16)
    kr = _apply_rope(kr_ref[...].astype(F32), cos_ref[...], sina_ref[...], sinb_ref[...])
    kr_t = kr.T.astype(BF16)
    for h in range(N_ATTN_HEADS):
        kn_t = lax.dot_general(wkt_ref[h * LANES:(h + 1) * LANES, :], c, (((1,), (1,)), ((), ())),
                               preferred_element_type=F32)
        kt_ref[0, h, 0, :LANES, :] = kn_t.astype(BF16)
        kt_ref[0, h, 0, LANES:, :] = kr_t
        v = jnp.dot(c, wv_ref[:, h * V_HEAD_DIM:(h + 1) * V_HEAD_DIM], preferred_element_type=F32)
        v_ref[0, h] = v.astype(BF16)


def _kv_projection(ckv, qx, kv_norm_w, w_kt, w_v, cos, sina, sinb, batch, seq):
    tk = ATTN_TK
    nk = seq // tk
    tab = pl.BlockSpec((tk, LANES), lambda i: (i, 0))
    return pl.pallas_call(
        _kvproj_kernel,
        grid=(batch * nk,),
        in_specs=[
            pl.BlockSpec((tk, KV_LORA), lambda i: (i, U_CKV // KV_LORA)),
            pl.BlockSpec((tk, LANES), lambda i: (i, (U_QX + Q_LORA) // LANES)),
            pl.BlockSpec((1, KV_LORA), lambda i: (0, 0)),
            pl.BlockSpec((N_ATTN_HEADS * QK_NOPE_DIM, KV_LORA), lambda i: (0, 0)),
            pl.BlockSpec((KV_LORA, N_ATTN_HEADS * V_HEAD_DIM), lambda i: (0, 0)),
            tab, tab, tab,
        ],
        out_specs=[
            pl.BlockSpec((1, N_ATTN_HEADS, 1, QK_PAD_DIM, tk), lambda i: (i // nk, 0, i % nk, 0, 0)),
            pl.BlockSpec((1, N_ATTN_HEADS, tk, V_HEAD_DIM), lambda i: (i // nk, 0, i % nk, 0)),
        ],
        out_shape=[
            jax.ShapeDtypeStruct((batch, N_ATTN_HEADS, nk, QK_PAD_DIM, tk), BF16),
            jax.ShapeDtypeStruct((batch, N_ATTN_HEADS, seq, V_HEAD_DIM), BF16),
        ],
        compiler_params=_cparams("parallel"),
        name="kv_projection",
    )(ckv, qx, kv_norm_w, w_kt, w_v, cos, sina, sinb)


def _attn_kernel(q_ref, kt_ref, v_ref, z_ref, o_ref, s_ref, p_ref, a_ref, m_ref, l_ref, acc_ref, *, tk, nk, unroll):
    m_ref[...] = jnp.full_like(m_ref, -jnp.inf)
    l_ref[...] = jnp.zeros_like(l_ref)
    acc_ref[...] = jnp.zeros_like(acc_ref)
    reps = tk // LANES

    def scores(j, slot):
        s_ref[slot] = jnp.dot(q_ref[0, 0], kt_ref[0, 0, j], preferred_element_type=F32)

    def softmax(slot):
        s = s_ref[slot]
        m_prev = m_ref[...]
        m_new = jnp.maximum(m_prev, jnp.max(s, axis=1, keepdims=True))
        alpha = jnp.exp2(m_prev - m_new)
        p = jnp.exp2(s - jnp.concatenate([m_new] * reps, axis=1))
        l_part = p[:, :LANES]
        for r in range(1, reps):
            l_part = l_part + p[:, r * LANES:(r + 1) * LANES]
        l_ref[...] = alpha * l_ref[...] + l_part
        m_ref[...] = m_new
        a_ref[slot] = alpha
        p_ref[slot] = p.astype(BF16)

    def values(j, slot):
        start = pl.multiple_of(j * tk, tk)
        pv = jnp.dot(p_ref[slot], v_ref[0, 0, pl.ds(start, tk), :], preferred_element_type=F32)
        acc_ref[...] = a_ref[slot] * acc_ref[...] + pv

    def step(j, parity, has_next=True, has_prev=True):
        if has_next:
            scores(j + 1, 1 - parity)
        softmax(parity)
        if has_prev:
            values(j - 1, 1 - parity)

    scores(0, 0)
    step(0, 0, has_prev=False)
    n_loop = (nk - 2) // unroll

    def group(i, carry):
        for u in range(unroll):
            step(1 + unroll * i + u, (1 + u) % 2)
        return carry

    lax.fori_loop(0, n_loop, group, 0)
    for j in range(1 + unroll * n_loop, nk):
        step(j, j % 2, has_next=j + 1 < nk)
    values(nk - 1, (nk - 1) % 2)
    l = jnp.sum(l_ref[...], axis=1, keepdims=True)
    o_ref[...] = (acc_ref[...] / l * _silu(z_ref[...].astype(F32))).astype(BF16)


def _attention(q, kt, v, gates, batch, seq, tq, unroll):
    tk = ATTN_TK
    nk = seq // tk
    nq = seq // tq
    z_col0 = D_MODEL // V_HEAD_DIM
    return pl.pallas_call(
        functools.partial(_attn_kernel, tk=tk, nk=nk, unroll=unroll),
        grid=(batch, N_ATTN_HEADS, nq),
        in_specs=[
            pl.BlockSpec((1, 1, tq, QK_PAD_DIM), lambda b, h, i: (b, h, i, 0)),
            pl.BlockSpec((1, 1, nk, QK_PAD_DIM, tk), lambda b, h, i: (b, h, 0, 0, 0)),
            pl.BlockSpec((1, 1, seq, V_HEAD_DIM), lambda b, h, i: (b, h, 0, 0)),
            pl.BlockSpec((tq, V_HEAD_DIM), lambda b, h, i: (b * nq + i, z_col0 + h)),
        ],
        out_specs=pl.BlockSpec((tq, V_HEAD_DIM), lambda b, h, i: (b * nq + i, h)),
        out_shape=jax.ShapeDtypeStruct((batch * seq, D_ATTN), BF16),
        scratch_shapes=[pltpu.VMEM((2, tq, tk), F32), pltpu.VMEM((2, tq, tk), BF16),
                        pltpu.VMEM((2, tq, LANES), F32),
                        pltpu.VMEM((tq, LANES), F32), pltpu.VMEM((tq, LANES), F32),
                        pltpu.VMEM((tq, V_HEAD_DIM), F32)],
        compiler_params=_cparams("parallel", "parallel", "arbitrary"),
        name="attention",
    )(q, kt, v, gates)


def _merge_kernel(y_ref, o_ref, gs_ref, ga_ref, wps_ref, wpa_ref, out_ref):
    a = jnp.dot(y_ref[...], wps_ref[...], preferred_element_type=F32)
    merged = _sigmoid(gs_ref[...].astype(F32)) * a
    b = jnp.dot(o_ref[...], wpa_ref[...], preferred_element_type=F32)
    merged = merged + _sigmoid(ga_ref[...].astype(F32)) * b
    out_ref[...] = merged.astype(BF16)


def _merge(y, o, gates, w_ps, w_pa, tm):
    n = y.shape[0]
    row = pl.BlockSpec((tm, D_MODEL), lambda i: (i, 0))
    wspec = pl.BlockSpec((D_MODEL, D_MODEL), lambda i: (0, 0))
    return pl.pallas_call(
        _merge_kernel,
        grid=(n // tm,),
        in_specs=[row, row,
                  pl.BlockSpec((tm, D_MODEL), lambda i: (i, 2)),
                  pl.BlockSpec((tm, D_MODEL), lambda i: (i, 3)),
                  wspec, wspec],
        out_specs=row,
        out_shape=jax.ShapeDtypeStruct((n, D_MODEL), BF16),
        compiler_params=_cparams("parallel"),
        name="gated_merge",
    )(y, o, gates, gates, w_ps, w_pa)


def _out_kernel(m_ref, x_ref, w_ref, nw_ref, out_ref):
    r = x_ref[...] + jnp.dot(m_ref[...], w_ref[...], preferred_element_type=F32)
    out_ref[...] = _rms(r, nw_ref[...])


def _out_projection(merged, x2, w_out, final_w, tm):
    n = x2.shape[0]
    row = pl.BlockSpec((tm, D_MODEL), lambda i: (i, 0))
    return pl.pallas_call(
        _out_kernel,
        grid=(n // tm,),
        in_specs=[row, row, pl.BlockSpec((D_MODEL, D_MODEL), lambda i: (0, 0)),
                  pl.BlockSpec((1, D_MODEL), lambda i: (0, 0))],
        out_specs=row,
        out_shape=jax.ShapeDtypeStruct((n, D_MODEL), F32),
        compiler_params=_cparams("parallel"),
        name="out_projection",
    )(merged, x2, w_out, final_w)


def _permute_w_in(w_in):
    sizes = (D_SSM, D_XBC, N_SSM_HEADS, N_SSM_HEADS, Q_LORA, KV_LORA, QK_ROPE_DIM, D_ATTN, D_MODEL, D_MODEL)
    offs = [0]
    for sz in sizes:
        offs.append(offs[-1] + sz)
    z_ssm, xbc, dt_f, dt_b, q_a, c_kv, k_r, z_attn, g_ssm, g_attn = (
        w_in[:, offs[k]:offs[k + 1]] for k in range(len(sizes)))
    pad64 = jnp.zeros((D_MODEL, LANES - QK_ROPE_DIM), w_in.dtype)
    cols = [z_ssm, z_attn, g_ssm, g_attn, q_a, k_r, pad64, dt_f, dt_b, pad64, xbc, c_kv]
    return jnp.concatenate(cols, axis=1).astype(BF16)


def _permute_w_uq(w_uq):
    w = w_uq.reshape(Q_LORA, N_ATTN_HEADS, QK_NOPE_DIM + QK_ROPE_DIM)
    pad = jnp.zeros((Q_LORA, N_ATTN_HEADS, QK_PAD_DIM - QK_NOPE_DIM - QK_ROPE_DIM), w.dtype)
    return jnp.concatenate([w, pad], axis=2).reshape(Q_LORA, N_ATTN_HEADS * QK_PAD_DIM).astype(BF16)


def _layer(x2, tabs, batch, seq, norm_w, w_in, conv_w, conv_b, a_log_fwd, a_log_bwd, dt_bias_fwd, dt_bias_bwd,
           d_skip, ssm_norm_w, q_norm_w, w_uq, kv_norm_w, w_ukv, w_proj_ssm, w_proj_attn, w_out):
    cos, sina, sinb = tabs
    row = lambda v: v.reshape(1, -1).astype(F32)
    pad_row = lambda f, b: jnp.concatenate([f, b, jnp.zeros((LANES - 2 * N_SSM_HEADS,), F32)]).reshape(1, LANES)

    u, dt = _in_projection(x2, row(norm_w), _permute_w_in(w_in), tm=1024)

    conv_w8 = jnp.concatenate([conv_w.astype(F32), jnp.zeros((4, D_XBC), F32)], axis=0)
    xc = _conv_silu(u.reshape(batch, seq, U_WIDTH), conv_w8, row(conv_b), tm=512, tc=512)
    y = _ssd(xc.reshape(batch * seq, D_XBC), dt, pad_row(dt_bias_fwd, dt_bias_bwd), pad_row(a_log_fwd, a_log_bwd),
             u, row(jnp.repeat(d_skip, SSM_HEAD_DIM)), row(ssm_norm_w), batch, seq)

    w_kv = w_ukv.reshape(KV_LORA, N_ATTN_HEADS, QK_NOPE_DIM + V_HEAD_DIM)
    w_kt = w_kv[:, :, :QK_NOPE_DIM].reshape(KV_LORA, N_ATTN_HEADS * QK_NOPE_DIM).T.astype(BF16)
    w_v = w_kv[:, :, QK_NOPE_DIM:].reshape(KV_LORA, N_ATTN_HEADS * V_HEAD_DIM).astype(BF16)
    q = _q_projection(u, row(q_norm_w), _permute_w_uq(w_uq), cos, sina, sinb, batch, seq, tm=512)
    kt, v = _kv_projection(u, u, row(kv_norm_w), w_kt, w_v, cos, sina, sinb, batch, seq)
    o = _attention(q, kt, v, u, batch, seq, tq=512, unroll=10)

    merged = _merge(y, o, u, w_proj_ssm.astype(BF16), w_proj_attn.astype(BF16), tm=256)
    return merged, w_out.astype(BF16)


def kernel(x, positions, norm_w, w_in, conv_w, conv_b, a_log_fwd, a_log_bwd, dt_bias_fwd, dt_bias_bwd, d_skip,
           ssm_norm_w, q_norm_w, w_uq, kv_norm_w, w_ukv, w_proj_ssm, w_proj_attn, w_out, final_norm_w):
    batch, seq, _ = x.shape
    depth = norm_w.shape[0]
    assert depth == 1, "the single output projection is fused with the final norm"
    inv = 1.0 / (ROPE_THETA ** (jnp.arange(0, QK_ROPE_DIM, 2, dtype=F32) / QK_ROPE_DIM))
    inv_row = jnp.concatenate([inv, inv, jnp.zeros((LANES - QK_ROPE_DIM,), F32)]).reshape(1, LANES)
    tabs = _rope_tables(positions.reshape(batch * seq, 1), inv_row, tm=1024)
    x2 = x.reshape(batch * seq, D_MODEL)
    merged, w_o = _layer(x2, tabs, batch, seq, norm_w[0], w_in[0], conv_w[0], conv_b[0], a_log_fwd[0], a_log_bwd[0],
                         dt_bias_fwd[0], dt_bias_bwd[0], d_skip[0], ssm_norm_w[0], q_norm_w[0], w_uq[0],
                         kv_norm_w[0], w_ukv[0], w_proj_ssm[0], w_proj_attn[0], w_out[0])
    out = _out_projection(merged, x2, w_o, final_norm_w.reshape(1, -1).astype(F32), tm=512)
    return out.reshape(batch, seq, D_MODEL)
```

```python
import functools
import math

import jax
import jax.numpy as jnp
from jax import lax
from jax.experimental import pallas as pl
from jax.experimental.pallas import tpu as pltpu

F32 = jnp.float32
BF16 = jnp.bfloat16

D_MODEL = 2048
D_SSM = 2048
SSM_HEAD_DIM = 64
N_SSM_HEADS = 32
SSM_GROUPS = 4
HEADS_PER_GROUP = N_SSM_HEADS // SSM_GROUPS
D_STATE = 128
CHUNK = 128
D_XBC = D_SSM + 2 * SSM_GROUPS * D_STATE
N_ATTN_HEADS = 16
QK_NOPE_DIM = 128
QK_ROPE_DIM = 64
V_HEAD_DIM = 128
D_ATTN = N_ATTN_HEADS * V_HEAD_DIM
Q_LORA = 768
KV_LORA = 512
ROPE_THETA = 10000.0
EPS = 1e-6

LANES = 128
QK_PAD_DIM = 2 * LANES
QX_WIDTH = 1024
GATES_WIDTH = 4 * D_MODEL
W_IN_PERM_WIDTH = GATES_WIDTH + D_XBC + KV_LORA + QX_WIDTH

VMEM_LIMIT = 56 * 1024 * 1024


def _cparams(*sem, flags=None):
    return pltpu.CompilerParams(dimension_semantics=sem, vmem_limit_bytes=VMEM_LIMIT, flags=flags)


def _sigmoid(x):
    return 1.0 / (1.0 + jnp.exp(-x))


def _silu(x):
    return x * _sigmoid(x)


def _softplus(x):
    return jnp.maximum(x, 0.0) + jnp.log1p(jnp.exp(-jnp.abs(x)))


def _rms(x, w):
    ms = jnp.mean(x * x, axis=-1, keepdims=True)
    return x * lax.rsqrt(ms + EPS) * w


def _rope_kernel(pos_ref, inv_ref, pos_t_ref, inv_t_ref, cos_ref, sina_ref, sinb_ref, cos_t_ref, sin_t_ref):
    ang = pos_ref[...].astype(F32) * inv_ref[...]
    lane = lax.broadcasted_iota(jnp.int32, ang.shape, 1)
    c = jnp.cos(ang)
    s = jnp.sin(ang)
    cos_ref[...] = jnp.where(lane < 64, c, 0.0)
    sina_ref[...] = jnp.where(lane < 32, -s, 0.0)
    sinb_ref[...] = jnp.where((lane >= 32) & (lane < 64), s, 0.0)
    ang_t = inv_t_ref[...] * pos_t_ref[...].astype(F32)
    cos_t_ref[...] = jnp.cos(ang_t)
    sin_t_ref[...] = jnp.sin(ang_t)


def _rope_tables(pos_col, inv_row, pos_row, inv_col, tm):
    n = pos_col.shape[0]
    half = inv_col.shape[0]
    tab = jax.ShapeDtypeStruct((n, LANES), F32)
    tab_t = jax.ShapeDtypeStruct((half, n), F32)
    spec = pl.BlockSpec((tm, LANES), lambda i: (i, 0))
    spec_t = pl.BlockSpec((half, tm), lambda i: (0, i))
    return pl.pallas_call(
        _rope_kernel,
        grid=(n // tm,),
        in_specs=[pl.BlockSpec((tm, 1), lambda i: (i, 0)), pl.BlockSpec((1, LANES), lambda i: (0, 0)),
                  pl.BlockSpec((1, tm), lambda i: (0, i)), pl.BlockSpec((half, 1), lambda i: (0, 0))],
        out_specs=[spec, spec, spec, spec_t, spec_t],
        out_shape=[tab, tab, tab, tab_t, tab_t],
        compiler_params=_cparams("parallel"),
        name="rope_tables",
    )(pos_col, inv_row, pos_row, inv_col)


def _apply_rope(rp, cos, sina, sinb):
    return rp * cos + pltpu.roll(rp, 96, 1) * sina + pltpu.roll(rp, 32, 1) * sinb


IN_TN = 512
U_GATES = 0
U_QX = U_GATES + GATES_WIDTH
U_XBC = U_QX + QX_WIDTH
U_CKV = U_XBC + D_XBC
U_WIDTH = U_CKV + KV_LORA
_J_DT = (U_QX + QX_WIDTH - LANES) // IN_TN
assert (U_QX + QX_WIDTH) % IN_TN == 0 and U_WIDTH == W_IN_PERM_WIDTH


def _inproj_kernel(x_ref, nw_ref, w_ref, u_ref, dt_ref, h_ref):
    j = pl.program_id(1)

    @pl.when(j == 0)
    def _():
        h_ref[...] = _rms(x_ref[...], nw_ref[...]).astype(BF16)

    acc = jnp.dot(h_ref[...], w_ref[...], preferred_element_type=F32)
    u_ref[...] = acc.astype(BF16)

    @pl.when(j == _J_DT)
    def _():
        dt_ref[...] = acc[:, IN_TN - LANES:]


def _in_projection(x2, norm_w, w_perm, tm):
    n = x2.shape[0]
    grid = (n // tm, U_WIDTH // IN_TN)
    return pl.pallas_call(
        _inproj_kernel,
        grid=grid,
        in_specs=[
            pl.BlockSpec((tm, D_MODEL), lambda i, j: (i, 0)),
            pl.BlockSpec((1, D_MODEL), lambda i, j: (0, 0)),
            pl.BlockSpec((D_MODEL, IN_TN), lambda i, j: (0, j)),
        ],
        out_specs=[
            pl.BlockSpec((tm, IN_TN), lambda i, j: (i, j)),
            pl.BlockSpec((tm, LANES), lambda i, j: (i, 0)),
        ],
        out_shape=[
            jax.ShapeDtypeStruct((n, U_WIDTH), BF16),
            jax.ShapeDtypeStruct((n, LANES), F32),
        ],
        scratch_shapes=[pltpu.VMEM((tm, D_MODEL), BF16)],
        compiler_params=_cparams("parallel", "arbitrary"),
        name="in_projection",
    )(x2, norm_w, w_perm)


CONV_HALO = 16


def _conv_kernel(prev_ref, main_ref, next_ref, w_ref, b_ref, out_ref, ext_ref, *, tm):
    i = pl.program_id(1)
    last = pl.num_programs(1) - 1
    pm = jnp.where(i > 0, 1.0, 0.0).astype(F32)
    nm = jnp.where(i < last, 1.0, 0.0).astype(F32)
    ext_ref[0:CONV_HALO, :] = prev_ref[0].astype(F32) * pm
    ext_ref[CONV_HALO:CONV_HALO + tm, :] = main_ref[0].astype(F32)
    ext_ref[CONV_HALO + tm:2 * CONV_HALO + tm, :] = next_ref[0].astype(F32) * nm
    acc = b_ref[...] + w_ref[0:1, :] * ext_ref[CONV_HALO - 2:CONV_HALO - 2 + tm, :]
    acc = acc + w_ref[1:2, :] * ext_ref[CONV_HALO - 1:CONV_HALO - 1 + tm, :]
    acc = acc + w_ref[2:3, :] * ext_ref[CONV_HALO:CONV_HALO + tm, :]
    acc = acc + w_ref[3:4, :] * ext_ref[CONV_HALO + 1:CONV_HALO + 1 + tm, :]
    out_ref[0] = _silu(acc).astype(BF16)


def _conv_silu(u3, conv_w8, conv_b, tm, tc):
    b, s, _ = u3.shape
    hb = tm // CONV_HALO
    n_halo = s // CONV_HALO
    c0 = U_XBC // tc
    return pl.pallas_call(
        functools.partial(_conv_kernel, tm=tm),
        grid=(b, s // tm, D_XBC // tc),
        in_specs=[
            pl.BlockSpec((1, CONV_HALO, tc), lambda bi, i, ci: (bi, jnp.maximum(i * hb - 1, 0), c0 + ci)),
            pl.BlockSpec((1, tm, tc), lambda bi, i, ci: (bi, i, c0 + ci)),
            pl.BlockSpec((1, CONV_HALO, tc), lambda bi, i, ci: (bi, jnp.minimum((i + 1) * hb, n_halo - 1), c0 + ci)),
            pl.BlockSpec((8, tc), lambda bi, i, ci: (0, ci)),
            pl.BlockSpec((1, tc), lambda bi, i, ci: (0, ci)),
        ],
        out_specs=pl.BlockSpec((1, tm, tc), lambda bi, i, ci: (bi, i, ci)),
        out_shape=jax.ShapeDtypeStruct((b, s, D_XBC), BF16),
        scratch_shapes=[pltpu.VMEM((tm + 2 * CONV_HALO, tc), F32)],
        compiler_params=_cparams("parallel", "parallel", "parallel"),
        name="conv_silu",
    )(u3, u3, u3, conv_w8, conv_b)


def _ssd_chunk(direction, xc_ref, dt_ref, bias_ref, alog_ref, state_ref, y_ref):
    fwd = direction == 0
    xc = xc_ref[...]
    dtv = _softplus(dt_ref[...] + bias_ref[...])
    da = dtv * (-jnp.exp(alog_ref[...]))
    row = lax.broadcasted_iota(jnp.int32, (CHUNK, CHUNK), 0)
    col = lax.broadcasted_iota(jnp.int32, (CHUNK, CHUNK), 1)
    keep = (row >= col) if fwd else (row <= col)
    a_cs = jnp.dot(keep.astype(F32), da, precision=lax.Precision.HIGHEST, preferred_element_type=F32)
    a_cs_t = a_cs.T
    dt_t = dtv.T
    end = CHUNK - 1 if fwd else 0
    lane = lax.broadcasted_iota(jnp.int32, (1, LANES), 1)
    first_half = lane < SSM_HEAD_DIM

    for g in range(SSM_GROUPS):
        b_g = xc[:, D_SSM + g * D_STATE:D_SSM + (g + 1) * D_STATE]
        c_g = xc[:, D_SSM + SSM_GROUPS * D_STATE + g * D_STATE:D_SSM + SSM_GROUPS * D_STATE + (g + 1) * D_STATE]
        cb = lax.dot_general(c_g, b_g, (((1,), (1,)), ((), ())), preferred_element_type=F32)
        c_f = c_g.astype(F32)
        b_t = b_g.astype(F32).T
        for pair in range(HEADS_PER_GROUP // 2):
            p_idx = g * (HEADS_PER_GROUP // 2) + pair
            x_pair = xc[:, p_idx * LANES:(p_idx + 1) * LANES]
            s_old = state_ref[:, p_idx * LANES:(p_idx + 1) * LANES]
            rhs = jnp.concatenate([x_pair, s_old.astype(BF16)], axis=0)
            ys, news, decs = [], [], []
            for k in range(2):
                cidx = direction * N_SSM_HEADS + 2 * p_idx + k
                col_a = a_cs[:, cidx:cidx + 1]
                row_a = a_cs_t[cidx:cidx + 1, :]
                row_dt = dt_t[cidx:cidx + 1, :]
                decay = jnp.where(keep, jnp.exp(col_a - row_a), 0.0)
                m_h = (cb * decay * row_dt).astype(BF16)
                c_e = (c_f * jnp.exp(col_a)).astype(BF16)
                lhs = jnp.concatenate([m_h, c_e], axis=1)
                ys.append(jnp.dot(lhs, rhs, preferred_element_type=F32))
                tot = row_a[:, end:end + 1]
                w_row = jnp.exp(tot - row_a) * row_dt
                b_w = (b_t * w_row).astype(BF16)
                news.append(jnp.dot(b_w, x_pair, preferred_element_type=F32))
                decs.append(jnp.exp(tot))
            y_ref[:, p_idx * LANES:(p_idx + 1) * LANES] = jnp.where(first_half, ys[0], ys[1])
            dec = jnp.where(first_half, decs[0], decs[1])
            state_ref[:, p_idx * LANES:(p_idx + 1) * LANES] = s_old * dec + jnp.where(first_half, news[0], news[1])


def _ssd_fwd_kernel(xc_ref, dt_ref, bias_ref, alog_ref, yf_ref, state_ref, y_ref):
    @pl.when(pl.program_id(1) == 0)
    def _():
        state_ref[...] = jnp.zeros_like(state_ref)

    _ssd_chunk(0, xc_ref, dt_ref, bias_ref, alog_ref, state_ref, y_ref)
    yf_ref[...] = y_ref[...].astype(BF16)


def _ssd_bwd_kernel(xc_ref, dt_ref, bias_ref, alog_ref, yf_ref, z_ref, dskip_ref, nw_ref, out_ref, state_ref, y_ref):
    @pl.when(pl.program_id(1) == 0)
    def _():
        state_ref[...] = jnp.zeros_like(state_ref)

    _ssd_chunk(1, xc_ref, dt_ref, bias_ref, alog_ref, state_ref, y_ref)
    gw = D_SSM // SSM_GROUPS
    for g in range(SSM_GROUPS):
        sl = slice(g * gw, (g + 1) * gw)
        xs = xc_ref[:, sl].astype(F32)
        y = yf_ref[:, sl].astype(F32) + y_ref[:, sl] + dskip_ref[:, sl] * xs
        y = y * _silu(z_ref[:, sl].astype(F32))
        out_ref[:, sl] = _rms(y, nw_ref[:, sl]).astype(BF16)


def _ssd(xc2, dt2, bias_row, alog_row, gates, d_skip_row, ssm_norm_w, batch, seq):
    n = xc2.shape[0]
    nc = seq // CHUNK
    row_spec = lambda width: pl.BlockSpec((CHUNK, width), lambda b, c: (b * nc + c, 0))
    rev_spec = lambda width: pl.BlockSpec((CHUNK, width), lambda b, c: (b * nc + nc - 1 - c, 0))
    const = lambda width: pl.BlockSpec((1, width), lambda b, c: (0, 0))
    scratch = [pltpu.VMEM((D_STATE, D_SSM), F32), pltpu.VMEM((CHUNK, D_SSM), F32)]
    y_fwd = pl.pallas_call(
        _ssd_fwd_kernel,
        grid=(batch, nc),
        in_specs=[row_spec(D_XBC), row_spec(LANES), const(LANES), const(LANES)],
        out_specs=row_spec(D_SSM),
        out_shape=jax.ShapeDtypeStruct((n, D_SSM), BF16),
        scratch_shapes=scratch,
        compiler_params=_cparams("parallel", "arbitrary"),
        name="ssd_forward",
    )(xc2, dt2, bias_row, alog_row)
    return pl.pallas_call(
        _ssd_bwd_kernel,
        grid=(batch, nc),
        in_specs=[rev_spec(D_XBC), rev_spec(LANES), const(LANES), const(LANES), rev_spec(D_SSM),
                  rev_spec(D_SSM), const(D_SSM), const(D_SSM)],
        out_specs=rev_spec(D_SSM),
        out_shape=jax.ShapeDtypeStruct((n, D_SSM), BF16),
        scratch_shapes=scratch,
        compiler_params=_cparams("parallel", "arbitrary"),
        name="ssd_backward",
    )(xc2, dt2, bias_row, alog_row, y_fwd, gates, d_skip_row, ssm_norm_w)


Q_SCALE = (1.0 / math.sqrt(QK_NOPE_DIM + QK_ROPE_DIM)) * math.log2(math.e)


HALF_ROPE = QK_ROPE_DIM // 2
Q_HEAD_GROUP = 4


def _qproj_kernel(qx_ref, nw_ref, wn_ref, wr_ref, cos_ref, sin_ref, qt_ref):
    cq = _rms(qx_ref[:, :Q_LORA].astype(F32), nw_ref[...]).astype(BF16)
    nt = (((1,), (1,)), ((), ()))
    cos, sin = cos_ref[...], sin_ref[...]
    rope_t = lax.dot_general(wr_ref[...], cq, nt, preferred_element_type=F32)
    zeros = jnp.zeros((QK_PAD_DIM - QK_NOPE_DIM - QK_ROPE_DIM, cq.shape[0]), BF16)
    rows = Q_HEAD_GROUP * QK_NOPE_DIM
    for g in range(N_ATTN_HEADS // Q_HEAD_GROUP):
        nope_t = lax.dot_general(wn_ref[g * rows:(g + 1) * rows, :], cq, nt, preferred_element_type=F32)
        for k in range(Q_HEAD_GROUP):
            h = g * Q_HEAD_GROUP + k
            qt_ref[0, h, 0, :QK_NOPE_DIM, :] = (nope_t[k * QK_NOPE_DIM:(k + 1) * QK_NOPE_DIM, :] * Q_SCALE).astype(BF16)
            t1 = rope_t[h * QK_ROPE_DIM:h * QK_ROPE_DIM + HALF_ROPE, :]
            t2 = rope_t[h * QK_ROPE_DIM + HALF_ROPE:(h + 1) * QK_ROPE_DIM, :]
            r0 = QK_NOPE_DIM
            qt_ref[0, h, 0, r0:r0 + HALF_ROPE, :] = ((t1 * cos - t2 * sin) * Q_SCALE).astype(BF16)
            qt_ref[0, h, 0, r0 + HALF_ROPE:r0 + QK_ROPE_DIM, :] = ((t2 * cos + t1 * sin) * Q_SCALE).astype(BF16)
            qt_ref[0, h, 0, r0 + QK_ROPE_DIM:, :] = zeros


def _q_projection(u, q_norm_w, w_nope_t, w_rope_t, cos_t, sin_t, batch, seq, tm):
    nq = seq // tm
    tab = pl.BlockSpec((HALF_ROPE, tm), lambda i: (0, i))
    return pl.pallas_call(
        _qproj_kernel,
        grid=(batch * nq,),
        in_specs=[
            pl.BlockSpec((tm, QX_WIDTH), lambda i: (i, U_QX // QX_WIDTH)),
            pl.BlockSpec((1, Q_LORA), lambda i: (0, 0)),
            pl.BlockSpec((N_ATTN_HEADS * QK_NOPE_DIM, Q_LORA), lambda i: (0, 0)),
            pl.BlockSpec((N_ATTN_HEADS * QK_ROPE_DIM, Q_LORA), lambda i: (0, 0)),
            tab, tab,
        ],
        out_specs=pl.BlockSpec((1, N_ATTN_HEADS, 1, QK_PAD_DIM, tm), lambda i: (i // nq, 0, i % nq, 0, 0)),
        out_shape=jax.ShapeDtypeStruct((batch, N_ATTN_HEADS, nq, QK_PAD_DIM, tm), BF16),
        compiler_params=_cparams("parallel"),
        name="q_projection",
    )(u, q_norm_w, w_nope_t, w_rope_t, cos_t, sin_t)


ATTN_TK = 512
SOFTMAX_ROWS = 16
ATTN_SCORE_SLOTS = 3
ATTN_FLAGS = None
KV_HEAD_GROUP = 4


def _kvproj_kernel(ckv_ref, kr_ref, nw_ref, wk_ref, wvt_ref, cos_ref, sina_ref, sinb_ref, k_ref, vt_ref):
    c = _rms(ckv_ref[...].astype(F32), nw_ref[...]).astype(BF16)
    kr = _apply_rope(kr_ref[...].astype(F32), cos_ref[...], sina_ref[...], sinb_ref[...]).astype(BF16)
    nt = (((1,), (1,)), ((), ()))
    cols = KV_HEAD_GROUP * QK_NOPE_DIM
    for g in range(N_ATTN_HEADS // KV_HEAD_GROUP):
        kn = jnp.dot(c, wk_ref[:, g * cols:(g + 1) * cols], preferred_element_type=F32).astype(BF16)
        v_t = lax.dot_general(wvt_ref[g * cols:(g + 1) * cols, :], c, nt, preferred_element_type=F32).astype(BF16)
        for k in range(KV_HEAD_GROUP):
            h = g * KV_HEAD_GROUP + k
            k_ref[0, h, :, :QK_NOPE_DIM] = kn[:, k * QK_NOPE_DIM:(k + 1) * QK_NOPE_DIM]
            k_ref[0, h, :, QK_NOPE_DIM:] = kr
            vt_ref[0, h, 0] = v_t[k * V_HEAD_DIM:(k + 1) * V_HEAD_DIM, :]


def _kv_projection(u, kv_norm_w, w_k, w_vt, cos, sina, sinb, batch, seq):
    tk = ATTN_TK
    nk = seq // tk
    tab = pl.BlockSpec((tk, LANES), lambda i: (i, 0))
    return pl.pallas_call(
        _kvproj_kernel,
        grid=(batch * nk,),
        in_specs=[
            pl.BlockSpec((tk, KV_LORA), lambda i: (i, U_CKV // KV_LORA)),
            pl.BlockSpec((tk, LANES), lambda i: (i, (U_QX + Q_LORA) // LANES)),
            pl.BlockSpec((1, KV_LORA), lambda i: (0, 0)),
            pl.BlockSpec((KV_LORA, N_ATTN_HEADS * QK_NOPE_DIM), lambda i: (0, 0)),
            pl.BlockSpec((N_ATTN_HEADS * V_HEAD_DIM, KV_LORA), lambda i: (0, 0)),
            tab, tab, tab,
        ],
        out_specs=[
            pl.BlockSpec((1, N_ATTN_HEADS, tk, QK_PAD_DIM), lambda i: (i // nk, 0, i % nk, 0)),
            pl.BlockSpec((1, N_ATTN_HEADS, 1, V_HEAD_DIM, tk), lambda i: (i // nk, 0, i % nk, 0, 0)),
        ],
        out_shape=[
            jax.ShapeDtypeStruct((batch, N_ATTN_HEADS, seq, QK_PAD_DIM), BF16),
            jax.ShapeDtypeStruct((batch, N_ATTN_HEADS, nk, V_HEAD_DIM, tk), BF16),
        ],
        compiler_params=_cparams("parallel"),
        name="kv_projection",
    )(u, u, kv_norm_w, w_k, w_vt, cos, sina, sinb)


def _attn_kernel(qt_ref, k_ref, vt_ref, z_ref, o_ref, s_ref, p_ref, a_ref, m_ref, l_ref, acc_ref, *, tk, nk, unroll):
    n3 = s_ref.shape[0]
    n2 = p_ref.shape[0]
    period = n3 * n2 // math.gcd(n3, n2)
    rows = SOFTMAX_ROWS
    m_ref[n3 - 1] = jnp.full(m_ref.shape[1:], -jnp.inf, F32)
    l_ref[...] = jnp.zeros_like(l_ref)
    acc_ref[...] = jnp.zeros_like(acc_ref)

    def scores(j, slot):
        start = pl.multiple_of(j * tk, tk)
        s_ref[slot] = jnp.dot(k_ref[0, 0, pl.ds(start, tk), :], qt_ref[0, 0, 0],
                              preferred_element_type=F32)

    def running_max(slot, prev_slot):
        mx = s_ref[slot, 0:rows, :]
        for r in range(rows, tk, rows):
            mx = jnp.maximum(mx, s_ref[slot, r:r + rows, :])
        m_prev = m_ref[prev_slot]
        m_new = jnp.maximum(m_prev, jnp.max(mx, axis=0, keepdims=True))
        m_ref[slot] = m_new
        a_ref[slot] = jnp.exp2(m_prev - m_new)

    def exp_sum(slot, p_slot):
        m = m_ref[slot]
        psum = None
        for r in range(0, tk, rows):
            p = jnp.exp2(s_ref[slot, r:r + rows, :] - m)
            psum = p if psum is None else psum + p
            p_ref[p_slot, r:r + rows, :] = p.astype(BF16)
        l_ref[...] = a_ref[slot] * l_ref[...] + jnp.sum(psum, axis=0, keepdims=True)

    def values(j, p_slot, a_slot):
        pv = jnp.dot(vt_ref[0, 0, j], p_ref[p_slot], preferred_element_type=F32)
        acc_ref[...] = a_ref[a_slot] * acc_ref[...] + pv

    def step(j, jm):
        traced = not isinstance(j, int)
        if traced or j + 2 < nk:
            scores(j + 2, (jm + 2) % n3)
        if traced or j + 1 < nk:
            running_max((jm + 1) % n3, jm % n3)
        exp_sum(jm % n3, jm % n2)
        if traced or j >= 1:
            values(j - 1, (jm - 1) % n2, (jm - 1) % n3)

    assert unroll % period == 0, "buffer slots must be static in the loop body"
    scores(0, 0)
    scores(1, 1)
    running_max(0, n3 - 1)
    step(0, 0)
    n_loop = max(nk - 3, 0) // unroll

    def group(i, carry):
        for u in range(unroll):
            step(1 + unroll * i + u, (1 + u) % period)
        return carry

    lax.fori_loop(0, n_loop, group, 0)
    for j in range(1 + unroll * n_loop, nk):
        step(j, j % period)
    values(nk - 1, (nk - 1) % n2, (nk - 1) % n3)
    o_t = acc_ref[...] / l_ref[...]
    o_ref[...] = (o_t.T * _silu(z_ref[...].astype(F32))).astype(BF16)


def _attention(qt, k, vt, u, batch, seq, tq, unroll):
    tk = ATTN_TK
    nk = seq // tk
    nq = seq // tq
    z_col0 = (U_GATES + D_MODEL) // V_HEAD_DIM
    return pl.pallas_call(
        functools.partial(_attn_kernel, tk=tk, nk=nk, unroll=unroll),
        grid=(batch, N_ATTN_HEADS, nq),
        in_specs=[
            pl.BlockSpec((1, 1, 1, QK_PAD_DIM, tq), lambda b, h, i: (b, h, i, 0, 0)),
            pl.BlockSpec((1, 1, seq, QK_PAD_DIM), lambda b, h, i: (b, h, 0, 0)),
            pl.BlockSpec((1, 1, nk, V_HEAD_DIM, tk), lambda b, h, i: (b, h, 0, 0, 0)),
            pl.BlockSpec((tq, V_HEAD_DIM), lambda b, h, i: (b * nq + i, z_col0 + h)),
        ],
        out_specs=pl.BlockSpec((tq, V_HEAD_DIM), lambda b, h, i: (b * nq + i, h)),
        out_shape=jax.ShapeDtypeStruct((batch * seq, D_ATTN), BF16),
        scratch_shapes=[pltpu.VMEM((ATTN_SCORE_SLOTS, tk, tq), F32), pltpu.VMEM((2, tk, tq), BF16),
                        pltpu.VMEM((ATTN_SCORE_SLOTS, 1, tq), F32), pltpu.VMEM((ATTN_SCORE_SLOTS, 1, tq), F32),
                        pltpu.VMEM((1, tq), F32),
                        pltpu.VMEM((V_HEAD_DIM, tq), F32)],
        compiler_params=_cparams("parallel", "parallel", "arbitrary", flags=ATTN_FLAGS),
        name="attention",
    )(qt, k, vt, u)


def _merge_kernel(y_ref, o_ref, gs_ref, ga_ref, wps_ref, wpa_ref, out_ref):
    a = jnp.dot(y_ref[...], wps_ref[...], preferred_element_type=F32)
    merged = _sigmoid(gs_ref[...].astype(F32)) * a
    b = jnp.dot(o_ref[...], wpa_ref[...], preferred_element_type=F32)
    merged = merged + _sigmoid(ga_ref[...].astype(F32)) * b
    out_ref[...] = merged.astype(BF16)


def _merge(y, o, gates, w_ps, w_pa, tm):
    n = y.shape[0]
    row = pl.BlockSpec((tm, D_MODEL), lambda i: (i, 0))
    wspec = pl.BlockSpec((D_MODEL, D_MODEL), lambda i: (0, 0))
    return pl.pallas_call(
        _merge_kernel,
        grid=(n // tm,),
        in_specs=[row, row,
                  pl.BlockSpec((tm, D_MODEL), lambda i: (i, 2)),
                  pl.BlockSpec((tm, D_MODEL), lambda i: (i, 3)),
                  wspec, wspec],
        out_specs=row,
        out_shape=jax.ShapeDtypeStruct((n, D_MODEL), BF16),
        compiler_params=_cparams("parallel"),
        name="gated_merge",
    )(y, o, gates, gates, w_ps, w_pa)


def _out_kernel(m_ref, x_ref, w_ref, nw_ref, out_ref):
    r = x_ref[...] + jnp.dot(m_ref[...], w_ref[...], preferred_element_type=F32)
    out_ref[...] = _rms(r, nw_ref[...])


def _out_projection(merged, x2, w_out, final_w, tm):
    n = x2.shape[0]
    row = pl.BlockSpec((tm, D_MODEL), lambda i: (i, 0))
    return pl.pallas_call(
        _out_kernel,
        grid=(n // tm,),
        in_specs=[row, row, pl.BlockSpec((D_MODEL, D_MODEL), lambda i: (0, 0)),
                  pl.BlockSpec((1, D_MODEL), lambda i: (0, 0))],
        out_specs=row,
        out_shape=jax.ShapeDtypeStruct((n, D_MODEL), F32),
        compiler_params=_cparams("parallel"),
        name="out_projection",
    )(merged, x2, w_out, final_w)


def _permute_w_in(w_in):
    sizes = (D_SSM, D_XBC, N_SSM_HEADS, N_SSM_HEADS, Q_LORA, KV_LORA, QK_ROPE_DIM, D_ATTN, D_MODEL, D_MODEL)
    offs = [0]
    for sz in sizes:
        offs.append(offs[-1] + sz)
    z_ssm, xbc, dt_f, dt_b, q_a, c_kv, k_r, z_attn, g_ssm, g_attn = (
        w_in[:, offs[k]:offs[k + 1]] for k in range(len(sizes)))
    pad64 = jnp.zeros((D_MODEL, LANES - QK_ROPE_DIM), w_in.dtype)
    cols = [z_ssm, z_attn, g_ssm, g_attn, q_a, k_r, pad64, dt_f, dt_b, pad64, xbc, c_kv]
    return jnp.concatenate(cols, axis=1).astype(BF16)


def _split_w_uq(w_uq):
    w = w_uq.reshape(Q_LORA, N_ATTN_HEADS, QK_NOPE_DIM + QK_ROPE_DIM)
    w_nope = w[:, :, :QK_NOPE_DIM].reshape(Q_LORA, N_ATTN_HEADS * QK_NOPE_DIM)
    w_rope = w[:, :, QK_NOPE_DIM:].reshape(Q_LORA, N_ATTN_HEADS * QK_ROPE_DIM)
    return w_nope.T.astype(BF16), w_rope.T.astype(BF16)


def _layer(x2, tabs, batch, seq, norm_w, w_in, conv_w, conv_b, a_log_fwd, a_log_bwd, dt_bias_fwd, dt_bias_bwd,
           d_skip, ssm_norm_w, q_norm_w, w_uq, kv_norm_w, w_ukv, w_proj_ssm, w_proj_attn, w_out):
    cos, sina, sinb, cos_t, sin_t = tabs
    row = lambda v: v.reshape(1, -1).astype(F32)
    pad_row = lambda f, b: jnp.concatenate([f, b, jnp.zeros((LANES - 2 * N_SSM_HEADS,), F32)]).reshape(1, LANES)

    u, dt = _in_projection(x2, row(norm_w), _permute_w_in(w_in), tm=1024)

    conv_w8 = jnp.concatenate([conv_w.astype(F32), jnp.zeros((4, D_XBC), F32)], axis=0)
    xc = _conv_silu(u.reshape(batch, seq, U_WIDTH), conv_w8, row(conv_b), tm=512, tc=512)
    y = _ssd(xc.reshape(batch * seq, D_XBC), dt, pad_row(dt_bias_fwd, dt_bias_bwd), pad_row(a_log_fwd, a_log_bwd),
             u, row(jnp.repeat(d_skip, SSM_HEAD_DIM)), row(ssm_norm_w), batch, seq)

    w_kv = w_ukv.reshape(KV_LORA, N_ATTN_HEADS, QK_NOPE_DIM + V_HEAD_DIM)
    w_k = w_kv[:, :, :QK_NOPE_DIM].reshape(KV_LORA, N_ATTN_HEADS * QK_NOPE_DIM).astype(BF16)
    w_vt = w_kv[:, :, QK_NOPE_DIM:].reshape(KV_LORA, N_ATTN_HEADS * V_HEAD_DIM).T.astype(BF16)
    w_q_nope_t, w_q_rope_t = _split_w_uq(w_uq)
    tq = 512
    qt = _q_projection(u, row(q_norm_w), w_q_nope_t, w_q_rope_t, cos_t, sin_t, batch, seq, tm=tq)
    k, vt = _kv_projection(u, row(kv_norm_w), w_k, w_vt, cos, sina, sinb, batch, seq)
    o = _attention(qt, k, vt, u, batch, seq, tq=tq, unroll=12)

    merged = _merge(y, o, u, w_proj_ssm.astype(BF16), w_proj_attn.astype(BF16), tm=256)
    return merged, w_out.astype(BF16)


def kernel(x, positions, norm_w, w_in, conv_w, conv_b, a_log_fwd, a_log_bwd, dt_bias_fwd, dt_bias_bwd, d_skip,
           ssm_norm_w, q_norm_w, w_uq, kv_norm_w, w_ukv, w_proj_ssm, w_proj_attn, w_out, final_norm_w):
    batch, seq, _ = x.shape
    depth = norm_w.shape[0]
    assert depth == 1, "the single output projection is fused with the final norm"
    inv = 1.0 / (ROPE_THETA ** (jnp.arange(0, QK_ROPE_DIM, 2, dtype=F32) / QK_ROPE_DIM))
    inv_row = jnp.concatenate([inv, inv, jnp.zeros((LANES - QK_ROPE_DIM,), F32)]).reshape(1, LANES)
    tabs = _rope_tables(positions.reshape(batch * seq, 1), inv_row, positions.reshape(1, batch * seq),
                        inv.reshape(-1, 1), tm=1024)
    x2 = x.reshape(batch * seq, D_MODEL)
    merged, w_o = _layer(x2, tabs, batch, seq, norm_w[0], w_in[0], conv_w[0], conv_b[0], a_log_fwd[0], a_log_bwd[0],
                         dt_bias_fwd[0], dt_bias_bwd[0], d_skip[0], ssm_norm_w[0], q_norm_w[0], w_uq[0],
                         kv_norm_w[0], w_ukv[0], w_proj_ssm[0], w_proj_attn[0], w_out[0])
    out = _out_projection(merged, x2, w_o, final_norm_w.reshape(1, -1).astype(F32), tm=512)
    return out.reshape(batch, seq, D_MODEL)
```

```python
import functools
import math

import jax
import jax.numpy as jnp
from jax import lax
from jax.experimental import pallas as pl
from jax.experimental.pallas import tpu as pltpu

F32 = jnp.float32
BF16 = jnp.bfloat16

D_MODEL = 2048
D_SSM = 2048
SSM_HEAD_DIM = 64
N_SSM_HEADS = 32
SSM_GROUPS = 4
HEADS_PER_GROUP = N_SSM_HEADS // SSM_GROUPS
D_STATE = 128
CHUNK = 128
D_XBC = D_SSM + 2 * SSM_GROUPS * D_STATE
N_ATTN_HEADS = 16
QK_NOPE_DIM = 128
QK_ROPE_DIM = 64
V_HEAD_DIM = 128
D_ATTN = N_ATTN_HEADS * V_HEAD_DIM
Q_LORA = 768
KV_LORA = 512
ROPE_THETA = 10000.0
EPS = 1e-6

LANES = 128
QK_PAD_DIM = 2 * LANES
QX_WIDTH = 1024
GATES_WIDTH = 4 * D_MODEL
W_IN_PERM_WIDTH = GATES_WIDTH + D_XBC + KV_LORA + QX_WIDTH

VMEM_LIMIT = 56 * 1024 * 1024


def _cparams(*sem):
    return pltpu.CompilerParams(dimension_semantics=sem, vmem_limit_bytes=VMEM_LIMIT)


def _sigmoid(x):
    return 1.0 / (1.0 + jnp.exp(-x))


def _silu(x):
    return x * _sigmoid(x)


def _softplus(x):
    return jnp.maximum(x, 0.0) + jnp.log1p(jnp.exp(-jnp.abs(x)))


def _rms(x, w):
    ms = jnp.mean(x * x, axis=-1, keepdims=True)
    return x * lax.rsqrt(ms + EPS) * w


def _rope_kernel(pos_ref, inv_ref, cos_ref, sina_ref, sinb_ref):
    ang = pos_ref[...].astype(F32) * inv_ref[...]
    lane = lax.broadcasted_iota(jnp.int32, ang.shape, 1)
    c = jnp.cos(ang)
    s = jnp.sin(ang)
    cos_ref[...] = jnp.where(lane < 64, c, 0.0)
    sina_ref[...] = jnp.where(lane < 32, -s, 0.0)
    sinb_ref[...] = jnp.where((lane >= 32) & (lane < 64), s, 0.0)


def _rope_tables(pos_col, inv_row, tm):
    n = pos_col.shape[0]
    tab = jax.ShapeDtypeStruct((n, LANES), F32)
    spec = pl.BlockSpec((tm, LANES), lambda i: (i, 0))
    return pl.pallas_call(
        _rope_kernel,
        grid=(n // tm,),
        in_specs=[pl.BlockSpec((tm, 1), lambda i: (i, 0)), pl.BlockSpec((1, LANES), lambda i: (0, 0))],
        out_specs=[spec, spec, spec],
        out_shape=[tab, tab, tab],
        compiler_params=_cparams("parallel"),
        name="rope_tables",
    )(pos_col, inv_row)


def _apply_rope(rp, cos, sina, sinb):
    return rp * cos + pltpu.roll(rp, 96, 1) * sina + pltpu.roll(rp, 32, 1) * sinb


IN_TN = 512
U_GATES = 0
U_QX = U_GATES + GATES_WIDTH
U_XBC = U_QX + QX_WIDTH
U_CKV = U_XBC + D_XBC
U_WIDTH = U_CKV + KV_LORA
_J_DT = (U_QX + QX_WIDTH - LANES) // IN_TN
assert (U_QX + QX_WIDTH) % IN_TN == 0 and U_WIDTH == W_IN_PERM_WIDTH


def _inproj_kernel(x_ref, nw_ref, w_ref, u_ref, dt_ref, h_ref):
    j = pl.program_id(1)

    @pl.when(j == 0)
    def _():
        h_ref[...] = _rms(x_ref[...], nw_ref[...]).astype(BF16)

    acc = jnp.dot(h_ref[...], w_ref[...], preferred_element_type=F32)
    u_ref[...] = acc.astype(BF16)

    @pl.when(j == _J_DT)
    def _():
        dt_ref[...] = acc[:, IN_TN - LANES:]


def _in_projection(x2, norm_w, w_perm, tm):
    n = x2.shape[0]
    grid = (n // tm, U_WIDTH // IN_TN)
    return pl.pallas_call(
        _inproj_kernel,
        grid=grid,
        in_specs=[
            pl.BlockSpec((tm, D_MODEL), lambda i, j: (i, 0)),
            pl.BlockSpec((1, D_MODEL), lambda i, j: (0, 0)),
            pl.BlockSpec((D_MODEL, IN_TN), lambda i, j: (0, j)),
        ],
        out_specs=[
            pl.BlockSpec((tm, IN_TN), lambda i, j: (i, j)),
            pl.BlockSpec((tm, LANES), lambda i, j: (i, 0)),
        ],
        out_shape=[
            jax.ShapeDtypeStruct((n, U_WIDTH), BF16),
            jax.ShapeDtypeStruct((n, LANES), F32),
        ],
        scratch_shapes=[pltpu.VMEM((tm, D_MODEL), BF16)],
        compiler_params=_cparams("parallel", "arbitrary"),
        name="in_projection",
    )(x2, norm_w, w_perm)


CONV_HALO = 16


def _conv_kernel(prev_ref, main_ref, next_ref, w_ref, b_ref, out_ref, ext_ref, *, tm):
    i = pl.program_id(1)
    last = pl.num_programs(1) - 1
    pm = jnp.where(i > 0, 1.0, 0.0).astype(F32)
    nm = jnp.where(i < last, 1.0, 0.0).astype(F32)
    ext_ref[0:CONV_HALO, :] = prev_ref[0].astype(F32) * pm
    ext_ref[CONV_HALO:CONV_HALO + tm, :] = main_ref[0].astype(F32)
    ext_ref[CONV_HALO + tm:2 * CONV_HALO + tm, :] = next_ref[0].astype(F32) * nm
    acc = b_ref[...] + w_ref[0:1, :] * ext_ref[CONV_HALO - 2:CONV_HALO - 2 + tm, :]
    acc = acc + w_ref[1:2, :] * ext_ref[CONV_HALO - 1:CONV_HALO - 1 + tm, :]
    acc = acc + w_ref[2:3, :] * ext_ref[CONV_HALO:CONV_HALO + tm, :]
    acc = acc + w_ref[3:4, :] * ext_ref[CONV_HALO + 1:CONV_HALO + 1 + tm, :]
    out_ref[0] = _silu(acc).astype(BF16)


def _conv_silu(u3, conv_w8, conv_b, tm, tc):
    b, s, _ = u3.shape
    hb = tm // CONV_HALO
    n_halo = s // CONV_HALO
    c0 = U_XBC // tc
    return pl.pallas_call(
        functools.partial(_conv_kernel, tm=tm),
        grid=(b, s // tm, D_XBC // tc),
        in_specs=[
            pl.BlockSpec((1, CONV_HALO, tc), lambda bi, i, ci: (bi, jnp.maximum(i * hb - 1, 0), c0 + ci)),
            pl.BlockSpec((1, tm, tc), lambda bi, i, ci: (bi, i, c0 + ci)),
            pl.BlockSpec((1, CONV_HALO, tc), lambda bi, i, ci: (bi, jnp.minimum((i + 1) * hb, n_halo - 1), c0 + ci)),
            pl.BlockSpec((8, tc), lambda bi, i, ci: (0, ci)),
            pl.BlockSpec((1, tc), lambda bi, i, ci: (0, ci)),
        ],
        out_specs=pl.BlockSpec((1, tm, tc), lambda bi, i, ci: (bi, i, ci)),
        out_shape=jax.ShapeDtypeStruct((b, s, D_XBC), BF16),
        scratch_shapes=[pltpu.VMEM((tm + 2 * CONV_HALO, tc), F32)],
        compiler_params=_cparams("parallel", "parallel", "parallel"),
        name="conv_silu",
    )(u3, u3, u3, conv_w8, conv_b)


def _ssd_chunk(direction, xc_ref, dt_ref, bias_ref, alog_ref, state_ref, y_ref):
    fwd = direction == 0
    xc = xc_ref[...]
    dtv = _softplus(dt_ref[...] + bias_ref[...])
    da = dtv * (-jnp.exp(alog_ref[...]))
    row = lax.broadcasted_iota(jnp.int32, (CHUNK, CHUNK), 0)
    col = lax.broadcasted_iota(jnp.int32, (CHUNK, CHUNK), 1)
    keep = (row >= col) if fwd else (row <= col)
    a_cs = jnp.dot(keep.astype(F32), da, precision=lax.Precision.HIGHEST, preferred_element_type=F32)
    a_cs_t = a_cs.T
    dt_t = dtv.T
    end = CHUNK - 1 if fwd else 0
    lane = lax.broadcasted_iota(jnp.int32, (1, LANES), 1)
    first_half = lane < SSM_HEAD_DIM

    for g in range(SSM_GROUPS):
        b_g = xc[:, D_SSM + g * D_STATE:D_SSM + (g + 1) * D_STATE]
        c_g = xc[:, D_SSM + SSM_GROUPS * D_STATE + g * D_STATE:D_SSM + SSM_GROUPS * D_STATE + (g + 1) * D_STATE]
        cb = lax.dot_general(c_g, b_g, (((1,), (1,)), ((), ())), preferred_element_type=F32)
        c_f = c_g.astype(F32)
        b_t = b_g.astype(F32).T
        for pair in range(HEADS_PER_GROUP // 2):
            p_idx = g * (HEADS_PER_GROUP // 2) + pair
            x_pair = xc[:, p_idx * LANES:(p_idx + 1) * LANES]
            s_old = state_ref[:, p_idx * LANES:(p_idx + 1) * LANES]
            rhs = jnp.concatenate([x_pair, s_old.astype(BF16)], axis=0)
            ys, news, decs = [], [], []
            for k in range(2):
                cidx = direction * N_SSM_HEADS + 2 * p_idx + k
                col_a = a_cs[:, cidx:cidx + 1]
                row_a = a_cs_t[cidx:cidx + 1, :]
                row_dt = dt_t[cidx:cidx + 1, :]
                decay = jnp.where(keep, jnp.exp(col_a - row_a), 0.0)
                m_h = (cb * decay * row_dt).astype(BF16)
                c_e = (c_f * jnp.exp(col_a)).astype(BF16)
                lhs = jnp.concatenate([m_h, c_e], axis=1)
                ys.append(jnp.dot(lhs, rhs, preferred_element_type=F32))
                tot = row_a[:, end:end + 1]
                w_row = jnp.exp(tot - row_a) * row_dt
                b_w = (b_t * w_row).astype(BF16)
                news.append(jnp.dot(b_w, x_pair, preferred_element_type=F32))
                decs.append(jnp.exp(tot))
            y_ref[:, p_idx * LANES:(p_idx + 1) * LANES] = jnp.where(first_half, ys[0], ys[1])
            dec = jnp.where(first_half, decs[0], decs[1])
            state_ref[:, p_idx * LANES:(p_idx + 1) * LANES] = s_old * dec + jnp.where(first_half, news[0], news[1])


def _ssd_fwd_kernel(xc_ref, dt_ref, bias_ref, alog_ref, yf_ref, state_ref, y_ref):
    @pl.when(pl.program_id(0) == 0)
    def _():
        state_ref[...] = jnp.zeros_like(state_ref)

    for b in range(xc_ref.shape[0]):
        _ssd_chunk(0, xc_ref.at[b], dt_ref.at[b], bias_ref, alog_ref, state_ref.at[b], y_ref.at[b])
        yf_ref[b] = y_ref[b].astype(BF16)


def _ssd_bwd_kernel(xc_ref, dt_ref, bias_ref, alog_ref, yf_ref, z_ref, dskip_ref, nw_ref, out_ref, state_ref, y_ref):
    @pl.when(pl.program_id(0) == 0)
    def _():
        state_ref[...] = jnp.zeros_like(state_ref)

    gw = D_SSM // SSM_GROUPS
    for b in range(xc_ref.shape[0]):
        _ssd_chunk(1, xc_ref.at[b], dt_ref.at[b], bias_ref, alog_ref, state_ref.at[b], y_ref.at[b])
        for g in range(SSM_GROUPS):
            sl = slice(g * gw, (g + 1) * gw)
            xs = xc_ref[b, :, sl].astype(F32)
            y = yf_ref[b, :, sl].astype(F32) + y_ref[b, :, sl] + dskip_ref[:, sl] * xs
            y = y * _silu(z_ref[b, :, sl].astype(F32))
            out_ref[b, :, sl] = _rms(y, nw_ref[:, sl]).astype(BF16)


def _ssd(xc3, dt3, bias_row, alog_row, u3, d_skip_row, ssm_norm_w):
    batch, seq, _ = xc3.shape
    nc = seq // CHUNK
    row_spec = lambda width: pl.BlockSpec((batch, CHUNK, width), lambda c: (0, c, 0))
    rev_spec = lambda width: pl.BlockSpec((batch, CHUNK, width), lambda c: (0, nc - 1 - c, 0))
    const = lambda width: pl.BlockSpec((1, width), lambda c: (0, 0))
    scratch = [pltpu.VMEM((batch, D_STATE, D_SSM), F32), pltpu.VMEM((batch, CHUNK, D_SSM), F32)]
    out_shape = jax.ShapeDtypeStruct((batch, seq, D_SSM), BF16)
    y_fwd = pl.pallas_call(
        _ssd_fwd_kernel,
        grid=(nc,),
        in_specs=[row_spec(D_XBC), row_spec(LANES), const(LANES), const(LANES)],
        out_specs=row_spec(D_SSM),
        out_shape=out_shape,
        scratch_shapes=scratch,
        compiler_params=_cparams("arbitrary"),
        name="ssd_forward",
    )(xc3, dt3, bias_row, alog_row)
    return pl.pallas_call(
        _ssd_bwd_kernel,
        grid=(nc,),
        in_specs=[rev_spec(D_XBC), rev_spec(LANES), const(LANES), const(LANES), rev_spec(D_SSM),
                  rev_spec(D_SSM), const(D_SSM), const(D_SSM)],
        out_specs=rev_spec(D_SSM),
        out_shape=out_shape,
        scratch_shapes=scratch,
        compiler_params=_cparams("arbitrary"),
        name="ssd_backward",
    )(xc3, dt3, bias_row, alog_row, y_fwd, u3, d_skip_row, ssm_norm_w)


Q_SCALE = (1.0 / math.sqrt(QK_NOPE_DIM + QK_ROPE_DIM)) * math.log2(math.e)


def _qproj_kernel(qx_ref, nw_ref, w_ref, cos_ref, sina_ref, sinb_ref, q_ref):
    cq = _rms(qx_ref[:, :Q_LORA].astype(F32), nw_ref[...]).astype(BF16)
    cos, sina, sinb = cos_ref[...], sina_ref[...], sinb_ref[...]
    for h in range(N_ATTN_HEADS):
        a = jnp.dot(cq, w_ref[:, h * QK_PAD_DIM:(h + 1) * QK_PAD_DIM], preferred_element_type=F32)
        q_ref[0, h, :, :LANES] = (a[:, :LANES] * Q_SCALE).astype(BF16)
        q_ref[0, h, :, LANES:] = (_apply_rope(a[:, LANES:], cos, sina, sinb) * Q_SCALE).astype(BF16)


def _q_projection(u, q_norm_w, w_q, cos, sina, sinb, batch, seq, tm):
    nq = seq // tm
    tab = pl.BlockSpec((tm, LANES), lambda i: (i, 0))
    return pl.pallas_call(
        _qproj_kernel,
        grid=(batch * nq,),
        in_specs=[
            pl.BlockSpec((tm, QX_WIDTH), lambda i: (i, U_QX // QX_WIDTH)),
            pl.BlockSpec((1, Q_LORA), lambda i: (0, 0)),
            pl.BlockSpec((Q_LORA, N_ATTN_HEADS * QK_PAD_DIM), lambda i: (0, 0)),
            tab, tab, tab,
        ],
        out_specs=pl.BlockSpec((1, N_ATTN_HEADS, tm, QK_PAD_DIM), lambda i: (i // nq, 0, i % nq, 0)),
        out_shape=jax.ShapeDtypeStruct((batch, N_ATTN_HEADS, seq, QK_PAD_DIM), BF16),
        compiler_params=_cparams("parallel"),
        name="q_projection",
    )(u, q_norm_w, w_q, cos, sina, sinb)


ATTN_TK = 512
ATTN_SLOTS = 2
KV_HEAD_GROUP = 4


def _kvproj_kernel(ckv_ref, kr_ref, nw_ref, wkt_ref, wv_ref, cos_ref, sina_ref, sinb_ref, kt_ref, v_ref):
    c = _rms(ckv_ref[...].astype(F32), nw_ref[...]).astype(BF16)
    kr = _apply_rope(kr_ref[...].astype(F32), cos_ref[...], sina_ref[...], sinb_ref[...])
    kr_t = kr.T.astype(BF16)
    nt = (((1,), (1,)), ((), ()))
    cols = KV_HEAD_GROUP * QK_NOPE_DIM
    for g in range(N_ATTN_HEADS // KV_HEAD_GROUP):
        kn_t = lax.dot_general(wkt_ref[g * cols:(g + 1) * cols, :], c, nt, preferred_element_type=F32).astype(BF16)
        v = jnp.dot(c, wv_ref[:, g * cols:(g + 1) * cols], preferred_element_type=F32).astype(BF16)
        for k in range(KV_HEAD_GROUP):
            h = g * KV_HEAD_GROUP + k
            kt_ref[0, h, 0, :QK_NOPE_DIM, :] = kn_t[k * QK_NOPE_DIM:(k + 1) * QK_NOPE_DIM, :]
            kt_ref[0, h, 0, QK_NOPE_DIM:, :] = kr_t
            v_ref[0, h] = v[:, k * V_HEAD_DIM:(k + 1) * V_HEAD_DIM]


def _kv_projection(u, kv_norm_w, w_kt, w_v, cos, sina, sinb, batch, seq):
    tk = ATTN_TK
    nk = seq // tk
    tab = pl.BlockSpec((tk, LANES), lambda i: (i, 0))
    return pl.pallas_call(
        _kvproj_kernel,
        grid=(batch * nk,),
        in_specs=[
            pl.BlockSpec((tk, KV_LORA), lambda i: (i, U_CKV // KV_LORA)),
            pl.BlockSpec((tk, LANES), lambda i: (i, (U_QX + Q_LORA) // LANES)),
            pl.BlockSpec((1, KV_LORA), lambda i: (0, 0)),
            pl.BlockSpec((N_ATTN_HEADS * QK_NOPE_DIM, KV_LORA), lambda i: (0, 0)),
            pl.BlockSpec((KV_LORA, N_ATTN_HEADS * V_HEAD_DIM), lambda i: (0, 0)),
            tab, tab, tab,
        ],
        out_specs=[
            pl.BlockSpec((1, N_ATTN_HEADS, 1, QK_PAD_DIM, tk), lambda i: (i // nk, 0, i % nk, 0, 0)),
            pl.BlockSpec((1, N_ATTN_HEADS, tk, V_HEAD_DIM), lambda i: (i // nk, 0, i % nk, 0)),
        ],
        out_shape=[
            jax.ShapeDtypeStruct((batch, N_ATTN_HEADS, nk, QK_PAD_DIM, tk), BF16),
            jax.ShapeDtypeStruct((batch, N_ATTN_HEADS, seq, V_HEAD_DIM), BF16),
        ],
        compiler_params=_cparams("parallel"),
        name="kv_projection",
    )(u, u, kv_norm_w, w_kt, w_v, cos, sina, sinb)


def _attn_kernel(q_ref, kt_ref, v_ref, z_ref, o_ref, s_ref, p_ref, a_ref, m_ref, l_ref, acc_ref, *, tk, nk, unroll):
    period = s_ref.shape[0]
    reps = tk // LANES
    m_ref[...] = jnp.full_like(m_ref, -jnp.inf)
    l_ref[...] = jnp.zeros_like(l_ref)
    acc_ref[...] = jnp.zeros_like(acc_ref)

    def scores(j, slot):
        s_ref[slot] = jnp.dot(q_ref[0, 0], kt_ref[0, 0, j], preferred_element_type=F32)

    def softmax(slot):
        s = s_ref[slot]
        m_prev = m_ref[...]
        m_new = jnp.maximum(m_prev, jnp.max(s, axis=1, keepdims=True))
        alpha = jnp.exp2(m_prev - m_new)
        p = jnp.exp2(s - jnp.concatenate([m_new] * reps, axis=1))
        l_part = p[:, :LANES]
        for r in range(1, reps):
            l_part = l_part + p[:, r * LANES:(r + 1) * LANES]
        l_ref[...] = alpha * l_ref[...] + l_part
        m_ref[...] = m_new
        a_ref[slot] = alpha
        p_ref[slot] = p.astype(BF16)

    def values(j, slot):
        start = pl.multiple_of(j * tk, tk)
        pv = jnp.dot(p_ref[slot], v_ref[0, 0, pl.ds(start, tk), :], preferred_element_type=F32)
        acc_ref[...] = a_ref[slot] * acc_ref[...] + pv

    def step(j, jm):
        traced = not isinstance(j, int)
        if traced or j + 1 < nk:
            scores(j + 1, (jm + 1) % period)
        softmax(jm % period)
        if traced or j >= 1:
            values(j - 1, (jm - 1) % period)

    assert unroll % period == 0, "buffer slots must be static in the loop body"
    scores(0, 0)
    step(0, 0)
    n_loop = max(nk - 2, 0) // unroll

    def group(i, carry):
        for u in range(unroll):
            step(1 + unroll * i + u, (1 + u) % period)
        return carry

    lax.fori_loop(0, n_loop, group, 0)
    for j in range(1 + unroll * n_loop, nk):
        step(j, j % period)
    values(nk - 1, (nk - 1) % period)
    l = jnp.sum(l_ref[...], axis=1, keepdims=True)
    o_ref[...] = (acc_ref[...] / l * _silu(z_ref[...].astype(F32))).astype(BF16)


def _attention(q, kt, v, u, batch, seq, tq, unroll):
    tk = ATTN_TK
    nk = seq // tk
    nq = seq // tq
    z_col0 = (U_GATES + D_MODEL) // V_HEAD_DIM
    stat = pltpu.VMEM((tq, LANES), F32)
    return pl.pallas_call(
        functools.partial(_attn_kernel, tk=tk, nk=nk, unroll=unroll),
        grid=(batch, N_ATTN_HEADS, nq),
        in_specs=[
            pl.BlockSpec((1, 1, tq, QK_PAD_DIM), lambda b, h, i: (b, h, i, 0)),
            pl.BlockSpec((1, 1, nk, QK_PAD_DIM, tk), lambda b, h, i: (b, h, 0, 0, 0)),
            pl.BlockSpec((1, 1, seq, V_HEAD_DIM), lambda b, h, i: (b, h, 0, 0)),
            pl.BlockSpec((tq, V_HEAD_DIM), lambda b, h, i: (b * nq + i, z_col0 + h)),
        ],
        out_specs=pl.BlockSpec((tq, V_HEAD_DIM), lambda b, h, i: (b * nq + i, h)),
        out_shape=jax.ShapeDtypeStruct((batch * seq, D_ATTN), BF16),
        scratch_shapes=[pltpu.VMEM((ATTN_SLOTS, tq, tk), F32), pltpu.VMEM((ATTN_SLOTS, tq, tk), BF16),
                        pltpu.VMEM((ATTN_SLOTS, tq, LANES), F32), stat, stat, pltpu.VMEM((tq, V_HEAD_DIM), F32)],
        compiler_params=_cparams("parallel", "parallel", "arbitrary"),
        name="attention",
    )(q, kt, v, u)


def _merge_kernel(y_ref, o_ref, gs_ref, ga_ref, wps_ref, wpa_ref, out_ref):
    a = jnp.dot(y_ref[...], wps_ref[...], preferred_element_type=F32)
    merged = _sigmoid(gs_ref[...].astype(F32)) * a
    b = jnp.dot(o_ref[...], wpa_ref[...], preferred_element_type=F32)
    merged = merged + _sigmoid(ga_ref[...].astype(F32)) * b
    out_ref[...] = merged.astype(BF16)


def _merge(y, o, gates, w_ps, w_pa, tm):
    n = y.shape[0]
    row = pl.BlockSpec((tm, D_MODEL), lambda i: (i, 0))
    wspec = pl.BlockSpec((D_MODEL, D_MODEL), lambda i: (0, 0))
    return pl.pallas_call(
        _merge_kernel,
        grid=(n // tm,),
        in_specs=[row, row,
                  pl.BlockSpec((tm, D_MODEL), lambda i: (i, 2)),
                  pl.BlockSpec((tm, D_MODEL), lambda i: (i, 3)),
                  wspec, wspec],
        out_specs=row,
        out_shape=jax.ShapeDtypeStruct((n, D_MODEL), BF16),
        compiler_params=_cparams("parallel"),
        name="gated_merge",
    )(y, o, gates, gates, w_ps, w_pa)


def _out_kernel(m_ref, x_ref, w_ref, nw_ref, out_ref):
    r = x_ref[...] + jnp.dot(m_ref[...], w_ref[...], preferred_element_type=F32)
    out_ref[...] = _rms(r, nw_ref[...])


def _out_projection(merged, x2, w_out, final_w, tm):
    n = x2.shape[0]
    row = pl.BlockSpec((tm, D_MODEL), lambda i: (i, 0))
    return pl.pallas_call(
        _out_kernel,
        grid=(n // tm,),
        in_specs=[row, row, pl.BlockSpec((D_MODEL, D_MODEL), lambda i: (0, 0)),
                  pl.BlockSpec((1, D_MODEL), lambda i: (0, 0))],
        out_specs=row,
        out_shape=jax.ShapeDtypeStruct((n, D_MODEL), F32),
        compiler_params=_cparams("parallel"),
        name="out_projection",
    )(merged, x2, w_out, final_w)


def _permute_w_in(w_in):
    sizes = (D_SSM, D_XBC, N_SSM_HEADS, N_SSM_HEADS, Q_LORA, KV_LORA, QK_ROPE_DIM, D_ATTN, D_MODEL, D_MODEL)
    offs = [0]
    for sz in sizes:
        offs.append(offs[-1] + sz)
    z_ssm, xbc, dt_f, dt_b, q_a, c_kv, k_r, z_attn, g_ssm, g_attn = (
        w_in[:, offs[k]:offs[k + 1]] for k in range(len(sizes)))
    pad64 = jnp.zeros((D_MODEL, LANES - QK_ROPE_DIM), w_in.dtype)
    cols = [z_ssm, z_attn, g_ssm, g_attn, q_a, k_r, pad64, dt_f, dt_b, pad64, xbc, c_kv]
    return jnp.concatenate(cols, axis=1).astype(BF16)


def _permute_w_uq(w_uq):
    w = w_uq.reshape(Q_LORA, N_ATTN_HEADS, QK_NOPE_DIM + QK_ROPE_DIM)
    pad = jnp.zeros((Q_LORA, N_ATTN_HEADS, QK_PAD_DIM - QK_NOPE_DIM - QK_ROPE_DIM), w.dtype)
    return jnp.concatenate([w, pad], axis=2).reshape(Q_LORA, N_ATTN_HEADS * QK_PAD_DIM).astype(BF16)


def _layer(x2, tabs, batch, seq, norm_w, w_in, conv_w, conv_b, a_log_fwd, a_log_bwd, dt_bias_fwd, dt_bias_bwd,
           d_skip, ssm_norm_w, q_norm_w, w_uq, kv_norm_w, w_ukv, w_proj_ssm, w_proj_attn, w_out):
    cos, sina, sinb = tabs
    row = lambda v: v.reshape(1, -1).astype(F32)
    pad_row = lambda f, b: jnp.concatenate([f, b, jnp.zeros((LANES - 2 * N_SSM_HEADS,), F32)]).reshape(1, LANES)

    u, dt = _in_projection(x2, row(norm_w), _permute_w_in(w_in), tm=1024)

    conv_w8 = jnp.concatenate([conv_w.astype(F32), jnp.zeros((4, D_XBC), F32)], axis=0)
    u3 = u.reshape(batch, seq, U_WIDTH)
    xc = _conv_silu(u3, conv_w8, row(conv_b), tm=512, tc=512)
    y = _ssd(xc, dt.reshape(batch, seq, LANES), pad_row(dt_bias_fwd, dt_bias_bwd), pad_row(a_log_fwd, a_log_bwd),
             u3, row(jnp.repeat(d_skip, SSM_HEAD_DIM)), row(ssm_norm_w)).reshape(batch * seq, D_SSM)

    w_kv = w_ukv.reshape(KV_LORA, N_ATTN_HEADS, QK_NOPE_DIM + V_HEAD_DIM)
    w_kt = w_kv[:, :, :QK_NOPE_DIM].reshape(KV_LORA, N_ATTN_HEADS * QK_NOPE_DIM).T.astype(BF16)
    w_v = w_kv[:, :, QK_NOPE_DIM:].reshape(KV_LORA, N_ATTN_HEADS * V_HEAD_DIM).astype(BF16)
    q = _q_projection(u, row(q_norm_w), _permute_w_uq(w_uq), cos, sina, sinb, batch, seq, tm=512)
    kt, v = _kv_projection(u, row(kv_norm_w), w_kt, w_v, cos, sina, sinb, batch, seq)
    o = _attention(q, kt, v, u, batch, seq, tq=512, unroll=14)

    merged = _merge(y, o, u, w_proj_ssm.astype(BF16), w_proj_attn.astype(BF16), tm=256)
    return merged, w_out.astype(BF16)


def kernel(x, positions, norm_w, w_in, conv_w, conv_b, a_log_fwd, a_log_bwd, dt_bias_fwd, dt_bias_bwd, d_skip,
           ssm_norm_w, q_norm_w, w_uq, kv_norm_w, w_ukv, w_proj_ssm, w_proj_attn, w_out, final_norm_w):
    batch, seq, _ = x.shape
    depth = norm_w.shape[0]
    assert depth == 1, "the single output projection is fused with the final norm"
    inv = 1.0 / (ROPE_THETA ** (jnp.arange(0, QK_ROPE_DIM, 2, dtype=F32) / QK_ROPE_DIM))
    inv_row = jnp.concatenate([inv, inv, jnp.zeros((LANES - QK_ROPE_DIM,), F32)]).reshape(1, LANES)
    tabs = _rope_tables(positions.reshape(batch * seq, 1), inv_row, tm=1024)
    x2 = x.reshape(batch * seq, D_MODEL)
    merged, w_o = _layer(x2, tabs, batch, seq, norm_w[0], w_in[0], conv_w[0], conv_b[0], a_log_fwd[0], a_log_bwd[0],
                         dt_bias_fwd[0], dt_bias_bwd[0], d_skip[0], ssm_norm_w[0], q_norm_w[0], w_uq[0],
                         kv_norm_w[0], w_ukv[0], w_proj_ssm[0], w_proj_attn[0], w_out[0])
    out = _out_projection(merged, x2, w_o, final_norm_w.reshape(1, -1).astype(F32), tm=512)
    return out.reshape(batch, seq, D_MODEL)
```

```python
import functools
import math

import jax
import jax.numpy as jnp
from jax import lax
from jax.experimental import pallas as pl
from jax.experimental.pallas import tpu as pltpu

F32 = jnp.float32
BF16 = jnp.bfloat16

D_MODEL = 2048
D_SSM = 2048
SSM_HEAD_DIM = 64
N_SSM_HEADS = 32
SSM_GROUPS = 4
HEADS_PER_GROUP = N_SSM_HEADS // SSM_GROUPS
D_STATE = 128
CHUNK = 128
D_XBC = D_SSM + 2 * SSM_GROUPS * D_STATE
N_ATTN_HEADS = 16
QK_NOPE_DIM = 128
QK_ROPE_DIM = 64
V_HEAD_DIM = 128
D_ATTN = N_ATTN_HEADS * V_HEAD_DIM
Q_LORA = 768
KV_LORA = 512
ROPE_THETA = 10000.0
EPS = 1e-6

LANES = 128
QK_PAD_DIM = 2 * LANES
QX_WIDTH = 1024
GATES_WIDTH = 4 * D_MODEL
W_IN_PERM_WIDTH = GATES_WIDTH + D_XBC + KV_LORA + QX_WIDTH

VMEM_LIMIT = 56 * 1024 * 1024


def _cparams(*sem):
    return pltpu.CompilerParams(dimension_semantics=sem, vmem_limit_bytes=VMEM_LIMIT)


def _sigmoid(x):
    return 1.0 / (1.0 + jnp.exp(-x))


def _silu(x):
    return x * _sigmoid(x)


def _softplus(x):
    return jnp.maximum(x, 0.0) + jnp.log1p(jnp.exp(-jnp.abs(x)))


def _rms(x, w):
    ms = jnp.mean(x * x, axis=-1, keepdims=True)
    return x * lax.rsqrt(ms + EPS) * w


def _rope_kernel(pos_ref, inv_ref, cos_ref, sina_ref, sinb_ref):
    ang = pos_ref[...].astype(F32) * inv_ref[...]
    lane = lax.broadcasted_iota(jnp.int32, ang.shape, 1)
    c = jnp.cos(ang)
    s = jnp.sin(ang)
    cos_ref[...] = jnp.where(lane < 64, c, 0.0)
    sina_ref[...] = jnp.where(lane < 32, -s, 0.0)
    sinb_ref[...] = jnp.where((lane >= 32) & (lane < 64), s, 0.0)


def _rope_tables(pos_col, inv_row, tm):
    n = pos_col.shape[0]
    tab = jax.ShapeDtypeStruct((n, LANES), F32)
    spec = pl.BlockSpec((tm, LANES), lambda i: (i, 0))
    return pl.pallas_call(
        _rope_kernel,
        grid=(n // tm,),
        in_specs=[pl.BlockSpec((tm, 1), lambda i: (i, 0)), pl.BlockSpec((1, LANES), lambda i: (0, 0))],
        out_specs=[spec, spec, spec],
        out_shape=[tab, tab, tab],
        compiler_params=_cparams("parallel"),
        name="rope_tables",
    )(pos_col, inv_row)


def _apply_rope(rp, cos, sina, sinb):
    return rp * cos + pltpu.roll(rp, 96, 1) * sina + pltpu.roll(rp, 32, 1) * sinb


IN_TN = 512
U_GATES = 0
U_QX = U_GATES + GATES_WIDTH
U_XBC = U_QX + QX_WIDTH
U_CKV = U_XBC + D_XBC
U_WIDTH = U_CKV + KV_LORA
_J_DT = (U_QX + QX_WIDTH - LANES) // IN_TN
assert (U_QX + QX_WIDTH) % IN_TN == 0 and U_WIDTH == W_IN_PERM_WIDTH


def _inproj_kernel(x_ref, nw_ref, w_ref, u_ref, dt_ref, h_ref):
    j = pl.program_id(1)

    @pl.when(j == 0)
    def _():
        h_ref[...] = _rms(x_ref[...], nw_ref[...]).astype(BF16)

    acc = jnp.dot(h_ref[...], w_ref[...], preferred_element_type=F32)
    u_ref[...] = acc.astype(BF16)

    @pl.when(j == _J_DT)
    def _():
        dt_ref[...] = acc[:, IN_TN - LANES:]


def _in_projection(x2, norm_w, w_perm, tm):
    n = x2.shape[0]
    grid = (n // tm, U_WIDTH // IN_TN)
    return pl.pallas_call(
        _inproj_kernel,
        grid=grid,
        in_specs=[
            pl.BlockSpec((tm, D_MODEL), lambda i, j: (i, 0)),
            pl.BlockSpec((1, D_MODEL), lambda i, j: (0, 0)),
            pl.BlockSpec((D_MODEL, IN_TN), lambda i, j: (0, j)),
        ],
        out_specs=[
            pl.BlockSpec((tm, IN_TN), lambda i, j: (i, j)),
            pl.BlockSpec((tm, LANES), lambda i, j: (i, 0)),
        ],
        out_shape=[
            jax.ShapeDtypeStruct((n, U_WIDTH), BF16),
            jax.ShapeDtypeStruct((n, LANES), F32),
        ],
        scratch_shapes=[pltpu.VMEM((tm, D_MODEL), BF16)],
        compiler_params=_cparams("parallel", "arbitrary"),
        name="in_projection",
    )(x2, norm_w, w_perm)


CONV_HALO = 16
CONV_SUB = 128
CONV_PAD = CONV_SUB // 2
CONV_SHIFTS = (-2, -1, 1)


def _conv_shift_matrix():
    t = jnp.arange(CONV_SUB)[:, None]
    r = jnp.arange(2 * CONV_SUB)[None, :]
    return jnp.concatenate([(r == CONV_PAD + t + sh) for sh in CONV_SHIFTS], axis=0).astype(BF16)


def _conv_kernel(prev_ref, main_ref, next_ref, sel_ref, w_ref, b_ref, out_ref, ext_ref, *, tm):
    i = pl.program_id(1)
    last = pl.num_programs(1) - 1
    lo = CONV_PAD - CONV_HALO
    hi = CONV_PAD + tm + CONV_HALO
    zeros = jnp.zeros((lo, ext_ref.shape[1]), BF16)
    ext_ref[0:lo, :] = zeros
    ext_ref[hi:hi + lo, :] = zeros
    ext_ref[lo:CONV_PAD, :] = jnp.where(i > 0, prev_ref[0], jnp.zeros_like(prev_ref[0]))
    ext_ref[CONV_PAD:CONV_PAD + tm, :] = main_ref[0]
    ext_ref[CONV_PAD + tm:hi, :] = jnp.where(i < last, next_ref[0], jnp.zeros_like(next_ref[0]))
    sel = sel_ref[...]
    for sb in range(tm // CONV_SUB):
        r0 = sb * CONV_SUB
        shifted = jnp.dot(sel, ext_ref[r0:r0 + 2 * CONV_SUB, :], preferred_element_type=F32)
        acc = b_ref[...] + w_ref[2:3, :] * main_ref[0, r0:r0 + CONV_SUB, :].astype(F32)
        for j, tap in enumerate((0, 1, 3)):
            acc = acc + w_ref[tap:tap + 1, :] * shifted[j * CONV_SUB:(j + 1) * CONV_SUB, :]
        out_ref[0, r0:r0 + CONV_SUB, :] = _silu(acc).astype(BF16)


def _conv_silu(u3, conv_w8, conv_b, tm, tc):
    b, s, _ = u3.shape
    hb = tm // CONV_HALO
    n_halo = s // CONV_HALO
    c0 = U_XBC // tc
    sel = _conv_shift_matrix()
    return pl.pallas_call(
        functools.partial(_conv_kernel, tm=tm),
        grid=(b, s // tm, D_XBC // tc),
        in_specs=[
            pl.BlockSpec((1, CONV_HALO, tc), lambda bi, i, ci: (bi, jnp.maximum(i * hb - 1, 0), c0 + ci)),
            pl.BlockSpec((1, tm, tc), lambda bi, i, ci: (bi, i, c0 + ci)),
            pl.BlockSpec((1, CONV_HALO, tc), lambda bi, i, ci: (bi, jnp.minimum((i + 1) * hb, n_halo - 1), c0 + ci)),
            pl.BlockSpec(sel.shape, lambda bi, i, ci: (0, 0)),
            pl.BlockSpec((8, tc), lambda bi, i, ci: (0, ci)),
            pl.BlockSpec((1, tc), lambda bi, i, ci: (0, ci)),
        ],
        out_specs=pl.BlockSpec((1, tm, tc), lambda bi, i, ci: (bi, i, ci)),
        out_shape=jax.ShapeDtypeStruct((b, s, D_XBC), BF16),
        scratch_shapes=[pltpu.VMEM((tm + 2 * CONV_PAD, tc), BF16)],
        compiler_params=_cparams("parallel", "parallel", "parallel"),
        name="conv_silu",
    )(u3, u3, u3, sel, conv_w8, conv_b)


def _ssd_chunk(direction, xc_ref, dt_ref, bias_ref, alog_ref, state_ref, y_ref):
    fwd = direction == 0
    xc = xc_ref[...]
    dtv = _softplus(dt_ref[...] + bias_ref[...])
    da = dtv * (-jnp.exp(alog_ref[...]))
    row = lax.broadcasted_iota(jnp.int32, (CHUNK, CHUNK), 0)
    col = lax.broadcasted_iota(jnp.int32, (CHUNK, CHUNK), 1)
    keep = (row >= col) if fwd else (row <= col)
    a_cs = jnp.dot(keep.astype(F32), da, precision=lax.Precision.HIGHEST, preferred_element_type=F32)
    a_cs_t = a_cs.T
    dt_t = dtv.T
    end = CHUNK - 1 if fwd else 0
    lane = lax.broadcasted_iota(jnp.int32, (1, LANES), 1)
    first_half = lane < SSM_HEAD_DIM

    for g in range(SSM_GROUPS):
        b_g = xc[:, D_SSM + g * D_STATE:D_SSM + (g + 1) * D_STATE]
        c_g = xc[:, D_SSM + SSM_GROUPS * D_STATE + g * D_STATE:D_SSM + SSM_GROUPS * D_STATE + (g + 1) * D_STATE]
        cb = lax.dot_general(c_g, b_g, (((1,), (1,)), ((), ())), preferred_element_type=F32)
        c_f = c_g.astype(F32)
        b_t = b_g.astype(F32).T
        for pair in range(HEADS_PER_GROUP // 2):
            p_idx = g * (HEADS_PER_GROUP // 2) + pair
            x_pair = xc[:, p_idx * LANES:(p_idx + 1) * LANES]
            s_old = state_ref[:, p_idx * LANES:(p_idx + 1) * LANES]
            rhs = jnp.concatenate([x_pair, s_old.astype(BF16)], axis=0)
            ys, news, decs = [], [], []
            for k in range(2):
                cidx = direction * N_SSM_HEADS + 2 * p_idx + k
                col_a = a_cs[:, cidx:cidx + 1]
                row_a = a_cs_t[cidx:cidx + 1, :]
                row_dt = dt_t[cidx:cidx + 1, :]
                decay = jnp.where(keep, jnp.exp(col_a - row_a), 0.0)
                m_h = (cb * decay * row_dt).astype(BF16)
                c_e = (c_f * jnp.exp(col_a)).astype(BF16)
                lhs = jnp.concatenate([m_h, c_e], axis=1)
                ys.append(jnp.dot(lhs, rhs, preferred_element_type=F32))
                tot = row_a[:, end:end + 1]
                w_row = jnp.exp(tot - row_a) * row_dt
                b_w = (b_t * w_row).astype(BF16)
                news.append(jnp.dot(b_w, x_pair, preferred_element_type=F32))
                decs.append(jnp.exp(tot))
            y_ref[:, p_idx * LANES:(p_idx + 1) * LANES] = jnp.where(first_half, ys[0], ys[1])
            dec = jnp.where(first_half, decs[0], decs[1])
            state_ref[:, p_idx * LANES:(p_idx + 1) * LANES] = s_old * dec + jnp.where(first_half, news[0], news[1])


def _ssd_fwd_kernel(xc_ref, dt_ref, bias_ref, alog_ref, yf_ref, state_ref, y_ref):
    @pl.when(pl.program_id(0) == 0)
    def _():
        state_ref[...] = jnp.zeros_like(state_ref)

    for b in range(xc_ref.shape[0]):
        _ssd_chunk(0, xc_ref.at[b], dt_ref.at[b], bias_ref, alog_ref, state_ref.at[b], y_ref.at[b])
        yf_ref[b] = y_ref[b].astype(BF16)


def _ssd_bwd_kernel(xc_ref, dt_ref, bias_ref, alog_ref, yf_ref, z_ref, dskip_ref, nw_ref, out_ref, state_ref, y_ref):
    @pl.when(pl.program_id(0) == 0)
    def _():
        state_ref[...] = jnp.zeros_like(state_ref)

    gw = D_SSM // SSM_GROUPS
    for b in range(xc_ref.shape[0]):
        _ssd_chunk(1, xc_ref.at[b], dt_ref.at[b], bias_ref, alog_ref, state_ref.at[b], y_ref.at[b])
        for g in range(SSM_GROUPS):
            sl = slice(g * gw, (g + 1) * gw)
            xs = xc_ref[b, :, sl].astype(F32)
            y = yf_ref[b, :, sl].astype(F32) + y_ref[b, :, sl] + dskip_ref[:, sl] * xs
            y = y * _silu(z_ref[b, :, sl].astype(F32))
            out_ref[b, :, sl] = _rms(y, nw_ref[:, sl]).astype(BF16)


def _ssd(xc3, dt3, bias_row, alog_row, u3, d_skip_row, ssm_norm_w):
    batch, seq, _ = xc3.shape
    nc = seq // CHUNK
    row_spec = lambda width: pl.BlockSpec((batch, CHUNK, width), lambda c: (0, c, 0))
    rev_spec = lambda width: pl.BlockSpec((batch, CHUNK, width), lambda c: (0, nc - 1 - c, 0))
    const = lambda width: pl.BlockSpec((1, width), lambda c: (0, 0))
    scratch = [pltpu.VMEM((batch, D_STATE, D_SSM), F32), pltpu.VMEM((batch, CHUNK, D_SSM), F32)]
    out_shape = jax.ShapeDtypeStruct((batch, seq, D_SSM), BF16)
    y_fwd = pl.pallas_call(
        _ssd_fwd_kernel,
        grid=(nc,),
        in_specs=[row_spec(D_XBC), row_spec(LANES), const(LANES), const(LANES)],
        out_specs=row_spec(D_SSM),
        out_shape=out_shape,
        scratch_shapes=scratch,
        compiler_params=_cparams("arbitrary"),
        name="ssd_forward",
    )(xc3, dt3, bias_row, alog_row)
    return pl.pallas_call(
        _ssd_bwd_kernel,
        grid=(nc,),
        in_specs=[rev_spec(D_XBC), rev_spec(LANES), const(LANES), const(LANES), rev_spec(D_SSM),
                  rev_spec(D_SSM), const(D_SSM), const(D_SSM)],
        out_specs=rev_spec(D_SSM),
        out_shape=out_shape,
        scratch_shapes=scratch,
        compiler_params=_cparams("arbitrary"),
        name="ssd_backward",
    )(xc3, dt3, bias_row, alog_row, y_fwd, u3, d_skip_row, ssm_norm_w)


Q_SCALE = (1.0 / math.sqrt(QK_NOPE_DIM + QK_ROPE_DIM)) * math.log2(math.e)


def _qproj_kernel(qx_ref, nw_ref, w_ref, cos_ref, sina_ref, sinb_ref, q_ref):
    cq = _rms(qx_ref[:, :Q_LORA].astype(F32), nw_ref[...]).astype(BF16)
    cos, sina, sinb = cos_ref[...], sina_ref[...], sinb_ref[...]
    for h in range(N_ATTN_HEADS):
        a = jnp.dot(cq, w_ref[:, h * QK_PAD_DIM:(h + 1) * QK_PAD_DIM], preferred_element_type=F32)
        q_ref[0, h, :, :LANES] = (a[:, :LANES] * Q_SCALE).astype(BF16)
        q_ref[0, h, :, LANES:] = (_apply_rope(a[:, LANES:], cos, sina, sinb) * Q_SCALE).astype(BF16)


def _q_projection(u, q_norm_w, w_q, cos, sina, sinb, batch, seq, tm):
    nq = seq // tm
    tab = pl.BlockSpec((tm, LANES), lambda i: (i, 0))
    return pl.pallas_call(
        _qproj_kernel,
        grid=(batch * nq,),
        in_specs=[
            pl.BlockSpec((tm, QX_WIDTH), lambda i: (i, U_QX // QX_WIDTH)),
            pl.BlockSpec((1, Q_LORA), lambda i: (0, 0)),
            pl.BlockSpec((Q_LORA, N_ATTN_HEADS * QK_PAD_DIM), lambda i: (0, 0)),
            tab, tab, tab,
        ],
        out_specs=pl.BlockSpec((1, N_ATTN_HEADS, tm, QK_PAD_DIM), lambda i: (i // nq, 0, i % nq, 0)),
        out_shape=jax.ShapeDtypeStruct((batch, N_ATTN_HEADS, seq, QK_PAD_DIM), BF16),
        compiler_params=_cparams("parallel"),
        name="q_projection",
    )(u, q_norm_w, w_q, cos, sina, sinb)


ATTN_TK = 512
ATTN_SLOTS = 2
KV_HEAD_GROUP = 4


def _kvproj_kernel(ckv_ref, kr_ref, nw_ref, wkt_ref, wv_ref, cos_ref, sina_ref, sinb_ref, kt_ref, v_ref):
    c = _rms(ckv_ref[...].astype(F32), nw_ref[...]).astype(BF16)
    kr = _apply_rope(kr_ref[...].astype(F32), cos_ref[...], sina_ref[...], sinb_ref[...])
    kr_t = kr.T.astype(BF16)
    nt = (((1,), (1,)), ((), ()))
    cols = KV_HEAD_GROUP * QK_NOPE_DIM
    for g in range(N_ATTN_HEADS // KV_HEAD_GROUP):
        kn_t = lax.dot_general(wkt_ref[g * cols:(g + 1) * cols, :], c, nt, preferred_element_type=F32).astype(BF16)
        v = jnp.dot(c, wv_ref[:, g * cols:(g + 1) * cols], preferred_element_type=F32).astype(BF16)
        for k in range(KV_HEAD_GROUP):
            h = g * KV_HEAD_GROUP + k
            kt_ref[0, h, 0, :QK_NOPE_DIM, :] = kn_t[k * QK_NOPE_DIM:(k + 1) * QK_NOPE_DIM, :]
            kt_ref[0, h, 0, QK_NOPE_DIM:, :] = kr_t
            v_ref[0, h] = v[:, k * V_HEAD_DIM:(k + 1) * V_HEAD_DIM]


def _kv_projection(u, kv_norm_w, w_kt, w_v, cos, sina, sinb, batch, seq):
    tk = ATTN_TK
    nk = seq // tk
    tab = pl.BlockSpec((tk, LANES), lambda i: (i, 0))
    return pl.pallas_call(
        _kvproj_kernel,
        grid=(batch * nk,),
        in_specs=[
            pl.BlockSpec((tk, KV_LORA), lambda i: (i, U_CKV // KV_LORA)),
            pl.BlockSpec((tk, LANES), lambda i: (i, (U_QX + Q_LORA) // LANES)),
            pl.BlockSpec((1, KV_LORA), lambda i: (0, 0)),
            pl.BlockSpec((N_ATTN_HEADS * QK_NOPE_DIM, KV_LORA), lambda i: (0, 0)),
            pl.BlockSpec((KV_LORA, N_ATTN_HEADS * V_HEAD_DIM), lambda i: (0, 0)),
            tab, tab, tab,
        ],
        out_specs=[
            pl.BlockSpec((1, N_ATTN_HEADS, 1, QK_PAD_DIM, tk), lambda i: (i // nk, 0, i % nk, 0, 0)),
            pl.BlockSpec((1, N_ATTN_HEADS, tk, V_HEAD_DIM), lambda i: (i // nk, 0, i % nk, 0)),
        ],
        out_shape=[
            jax.ShapeDtypeStruct((batch, N_ATTN_HEADS, nk, QK_PAD_DIM, tk), BF16),
            jax.ShapeDtypeStruct((batch, N_ATTN_HEADS, seq, V_HEAD_DIM), BF16),
        ],
        compiler_params=_cparams("parallel"),
        name="kv_projection",
    )(u, u, kv_norm_w, w_kt, w_v, cos, sina, sinb)


def _attn_kernel(q_ref, kt_ref, v_ref, z_ref, o_ref, s_ref, p_ref, a_ref, m_ref, l_ref, acc_ref, *, tk, nk, unroll):
    period = s_ref.shape[0]
    reps = tk // LANES
    m_ref[...] = jnp.full_like(m_ref, -jnp.inf)
    l_ref[...] = jnp.zeros_like(l_ref)
    acc_ref[...] = jnp.zeros_like(acc_ref)

    def scores(j, slot):
        s_ref[slot] = jnp.dot(q_ref[0, 0], kt_ref[0, 0, j], preferred_element_type=F32)

    def softmax(slot):
        s = s_ref[slot]
        m_prev = m_ref[...]
        m_new = jnp.maximum(m_prev, jnp.max(s, axis=1, keepdims=True))
        alpha = jnp.exp2(m_prev - m_new)
        p = jnp.exp2(s - jnp.concatenate([m_new] * reps, axis=1))
        l_part = p[:, :LANES]
        for r in range(1, reps):
            l_part = l_part + p[:, r * LANES:(r + 1) * LANES]
        l_ref[...] = alpha * l_ref[...] + l_part
        m_ref[...] = m_new
        a_ref[slot] = alpha
        p_ref[slot] = p.astype(BF16)

    def values(j, slot):
        start = pl.multiple_of(j * tk, tk)
        pv = jnp.dot(p_ref[slot], v_ref[0, 0, pl.ds(start, tk), :], preferred_element_type=F32)
        acc_ref[...] = a_ref[slot] * acc_ref[...] + pv

    def step(j, jm):
        traced = not isinstance(j, int)
        if traced or j + 1 < nk:
            scores(j + 1, (jm + 1) % period)
        softmax(jm % period)
        if traced or j >= 1:
            values(j - 1, (jm - 1) % period)

    assert unroll % period == 0, "buffer slots must be static in the loop body"
    scores(0, 0)
    step(0, 0)
    n_loop = max(nk - 2, 0) // unroll

    def group(i, carry):
        for u in range(unroll):
            step(1 + unroll * i + u, (1 + u) % period)
        return carry

    lax.fori_loop(0, n_loop, group, 0)
    for j in range(1 + unroll * n_loop, nk):
        step(j, j % period)
    values(nk - 1, (nk - 1) % period)
    l = jnp.sum(l_ref[...], axis=1, keepdims=True)
    o_ref[...] = (acc_ref[...] / l * _silu(z_ref[...].astype(F32))).astype(BF16)


def _attention(q, kt, v, u, batch, seq, tq, unroll):
    tk = ATTN_TK
    nk = seq // tk
    nq = seq // tq
    z_col0 = (U_GATES + D_MODEL) // V_HEAD_DIM
    stat = pltpu.VMEM((tq, LANES), F32)
    return pl.pallas_call(
        functools.partial(_attn_kernel, tk=tk, nk=nk, unroll=unroll),
        grid=(batch, N_ATTN_HEADS, nq),
        in_specs=[
            pl.BlockSpec((1, 1, tq, QK_PAD_DIM), lambda b, h, i: (b, h, i, 0)),
            pl.BlockSpec((1, 1, nk, QK_PAD_DIM, tk), lambda b, h, i: (b, h, 0, 0, 0)),
            pl.BlockSpec((1, 1, seq, V_HEAD_DIM), lambda b, h, i: (b, h, 0, 0)),
            pl.BlockSpec((tq, V_HEAD_DIM), lambda b, h, i: (b * nq + i, z_col0 + h)),
        ],
        out_specs=pl.BlockSpec((tq, V_HEAD_DIM), lambda b, h, i: (b * nq + i, h)),
        out_shape=jax.ShapeDtypeStruct((batch * seq, D_ATTN), BF16),
        scratch_shapes=[pltpu.VMEM((ATTN_SLOTS, tq, tk), F32), pltpu.VMEM((ATTN_SLOTS, tq, tk), BF16),
                        pltpu.VMEM((ATTN_SLOTS, tq, LANES), F32), stat, stat, pltpu.VMEM((tq, V_HEAD_DIM), F32)],
        compiler_params=_cparams("parallel", "parallel", "arbitrary"),
        name="attention",
    )(q, kt, v, u)


def _merge_kernel(y_ref, o_ref, gs_ref, ga_ref, wps_ref, wpa_ref, out_ref):
    a = jnp.dot(y_ref[...], wps_ref[...], preferred_element_type=F32)
    merged = _sigmoid(gs_ref[...].astype(F32)) * a
    b = jnp.dot(o_ref[...], wpa_ref[...], preferred_element_type=F32)
    merged = merged + _sigmoid(ga_ref[...].astype(F32)) * b
    out_ref[...] = merged.astype(BF16)


def _merge(y, o, gates, w_ps, w_pa, tm):
    n = y.shape[0]
    row = pl.BlockSpec((tm, D_MODEL), lambda i: (i, 0))
    wspec = pl.BlockSpec((D_MODEL, D_MODEL), lambda i: (0, 0))
    return pl.pallas_call(
        _merge_kernel,
        grid=(n // tm,),
        in_specs=[row, row,
                  pl.BlockSpec((tm, D_MODEL), lambda i: (i, 2)),
                  pl.BlockSpec((tm, D_MODEL), lambda i: (i, 3)),
                  wspec, wspec],
        out_specs=row,
        out_shape=jax.ShapeDtypeStruct((n, D_MODEL), BF16),
        compiler_params=_cparams("parallel"),
        name="gated_merge",
    )(y, o, gates, gates, w_ps, w_pa)


def _out_kernel(m_ref, x_ref, w_ref, nw_ref, out_ref):
    r = x_ref[...] + jnp.dot(m_ref[...], w_ref[...], preferred_element_type=F32)
    out_ref[...] = _rms(r, nw_ref[...])


def _out_projection(merged, x2, w_out, final_w, tm):
    n = x2.shape[0]
    row = pl.BlockSpec((tm, D_MODEL), lambda i: (i, 0))
    return pl.pallas_call(
        _out_kernel,
        grid=(n // tm,),
        in_specs=[row, row, pl.BlockSpec((D_MODEL, D_MODEL), lambda i: (0, 0)),
                  pl.BlockSpec((1, D_MODEL), lambda i: (0, 0))],
        out_specs=row,
        out_shape=jax.ShapeDtypeStruct((n, D_MODEL), F32),
        compiler_params=_cparams("parallel"),
        name="out_projection",
    )(merged, x2, w_out, final_w)


def _permute_w_in(w_in):
    sizes = (D_SSM, D_XBC, N_SSM_HEADS, N_SSM_HEADS, Q_LORA, KV_LORA, QK_ROPE_DIM, D_ATTN, D_MODEL, D_MODEL)
    offs = [0]
    for sz in sizes:
        offs.append(offs[-1] + sz)
    z_ssm, xbc, dt_f, dt_b, q_a, c_kv, k_r, z_attn, g_ssm, g_attn = (
        w_in[:, offs[k]:offs[k + 1]] for k in range(len(sizes)))
    pad64 = jnp.zeros((D_MODEL, LANES - QK_ROPE_DIM), w_in.dtype)
    cols = [z_ssm, z_attn, g_ssm, g_attn, q_a, k_r, pad64, dt_f, dt_b, pad64, xbc, c_kv]
    return jnp.concatenate(cols, axis=1).astype(BF16)


def _permute_w_uq(w_uq):
    w = w_uq.reshape(Q_LORA, N_ATTN_HEADS, QK_NOPE_DIM + QK_ROPE_DIM)
    pad = jnp.zeros((Q_LORA, N_ATTN_HEADS, QK_PAD_DIM - QK_NOPE_DIM - QK_ROPE_DIM), w.dtype)
    return jnp.concatenate([w, pad], axis=2).reshape(Q_LORA, N_ATTN_HEADS * QK_PAD_DIM).astype(BF16)


def _layer(x2, tabs, batch, seq, norm_w, w_in, conv_w, conv_b, a_log_fwd, a_log_bwd, dt_bias_fwd, dt_bias_bwd,
           d_skip, ssm_norm_w, q_norm_w, w_uq, kv_norm_w, w_ukv, w_proj_ssm, w_proj_attn, w_out):
    cos, sina, sinb = tabs
    row = lambda v: v.reshape(1, -1).astype(F32)
    pad_row = lambda f, b: jnp.concatenate([f, b, jnp.zeros((LANES - 2 * N_SSM_HEADS,), F32)]).reshape(1, LANES)

    u, dt = _in_projection(x2, row(norm_w), _permute_w_in(w_in), tm=1024)

    conv_w8 = jnp.concatenate([conv_w.astype(F32), jnp.zeros((4, D_XBC), F32)], axis=0)
    u3 = u.reshape(batch, seq, U_WIDTH)
    xc = _conv_silu(u3, conv_w8, row(conv_b), tm=512, tc=512)
    y = _ssd(xc, dt.reshape(batch, seq, LANES), pad_row(dt_bias_fwd, dt_bias_bwd), pad_row(a_log_fwd, a_log_bwd),
             u3, row(jnp.repeat(d_skip, SSM_HEAD_DIM)), row(ssm_norm_w)).reshape(batch * seq, D_SSM)

    w_kv = w_ukv.reshape(KV_LORA, N_ATTN_HEADS, QK_NOPE_DIM + V_HEAD_DIM)
    w_kt = w_kv[:, :, :QK_NOPE_DIM].reshape(KV_LORA, N_ATTN_HEADS * QK_NOPE_DIM).T.astype(BF16)
    w_v = w_kv[:, :, QK_NOPE_DIM:].reshape(KV_LORA, N_ATTN_HEADS * V_HEAD_DIM).astype(BF16)
    q = _q_projection(u, row(q_norm_w), _permute_w_uq(w_uq), cos, sina, sinb, batch, seq, tm=512)
    kt, v = _kv_projection(u, row(kv_norm_w), w_kt, w_v, cos, sina, sinb, batch, seq)
    o = _attention(q, kt, v, u, batch, seq, tq=512, unroll=14)

    merged = _merge(y, o, u, w_proj_ssm.astype(BF16), w_proj_attn.astype(BF16), tm=256)
    return merged, w_out.astype(BF16)


def kernel(x, positions, norm_w, w_in, conv_w, conv_b, a_log_fwd, a_log_bwd, dt_bias_fwd, dt_bias_bwd, d_skip,
           ssm_norm_w, q_norm_w, w_uq, kv_norm_w, w_ukv, w_proj_ssm, w_proj_attn, w_out, final_norm_w):
    batch, seq, _ = x.shape
    depth = norm_w.shape[0]
    assert depth == 1, "the single output projection is fused with the final norm"
    inv = 1.0 / (ROPE_THETA ** (jnp.arange(0, QK_ROPE_DIM, 2, dtype=F32) / QK_ROPE_DIM))
    inv_row = jnp.concatenate([inv, inv, jnp.zeros((LANES - QK_ROPE_DIM,), F32)]).reshape(1, LANES)
    tabs = _rope_tables(positions.reshape(batch * seq, 1), inv_row, tm=1024)
    x2 = x.reshape(batch * seq, D_MODEL)
    merged, w_o = _layer(x2, tabs, batch, seq, norm_w[0], w_in[0], conv_w[0], conv_b[0], a_log_fwd[0], a_log_bwd[0],
                         dt_bias_fwd[0], dt_bias_bwd[0], d_skip[0], ssm_norm_w[0], q_norm_w[0], w_uq[0],
                         kv_norm_w[0], w_ukv[0], w_proj_ssm[0], w_proj_attn[0], w_out[0])
    out = _out_projection(merged, x2, w_o, final_norm_w.reshape(1, -1).astype(F32), tm=512)
    return out.reshape(batch, seq, D_MODEL)
```

```python
import functools
import math

import jax
import jax.numpy as jnp
from jax import lax
from jax.experimental import pallas as pl
from jax.experimental.pallas import tpu as pltpu

F32 = jnp.float32
BF16 = jnp.bfloat16

D_MODEL = 2048
D_SSM = 2048
SSM_HEAD_DIM = 64
N_SSM_HEADS = 32
SSM_GROUPS = 4
HEADS_PER_GROUP = N_SSM_HEADS // SSM_GROUPS
D_STATE = 128
CHUNK = 128
D_XBC = D_SSM + 2 * SSM_GROUPS * D_STATE
N_ATTN_HEADS = 16
QK_NOPE_DIM = 128
QK_ROPE_DIM = 64
V_HEAD_DIM = 128
D_ATTN = N_ATTN_HEADS * V_HEAD_DIM
Q_LORA = 768
KV_LORA = 512
ROPE_THETA = 10000.0
EPS = 1e-6

LANES = 128
QK_PAD_DIM = 2 * LANES
QX_WIDTH = 1024
GATES_WIDTH = 4 * D_MODEL
W_IN_PERM_WIDTH = GATES_WIDTH + D_XBC + KV_LORA + QX_WIDTH

VMEM_LIMIT = 56 * 1024 * 1024


def _cparams(*sem):
    return pltpu.CompilerParams(dimension_semantics=sem, vmem_limit_bytes=VMEM_LIMIT)


def _sigmoid(x):
    return 1.0 / (1.0 + jnp.exp(-x))


def _silu(x):
    return x * _sigmoid(x)


def _softplus(x):
    return jnp.maximum(x, 0.0) + jnp.log1p(jnp.exp(-jnp.abs(x)))


def _rms(x, w):
    ms = jnp.mean(x * x, axis=-1, keepdims=True)
    return x * lax.rsqrt(ms + EPS) * w


def _rope_kernel(pos_ref, inv_ref, cos_ref, sina_ref, sinb_ref):
    ang = pos_ref[...].astype(F32) * inv_ref[...]
    lane = lax.broadcasted_iota(jnp.int32, ang.shape, 1)
    c = jnp.cos(ang)
    s = jnp.sin(ang)
    cos_ref[...] = jnp.where(lane < 64, c, 0.0)
    sina_ref[...] = jnp.where(lane < 32, -s, 0.0)
    sinb_ref[...] = jnp.where((lane >= 32) & (lane < 64), s, 0.0)


def _rope_tables(pos_col, inv_row, tm):
    n = pos_col.shape[0]
    tab = jax.ShapeDtypeStruct((n, LANES), F32)
    spec = pl.BlockSpec((tm, LANES), lambda i: (i, 0))
    return pl.pallas_call(
        _rope_kernel,
        grid=(n // tm,),
        in_specs=[pl.BlockSpec((tm, 1), lambda i: (i, 0)), pl.BlockSpec((1, LANES), lambda i: (0, 0))],
        out_specs=[spec, spec, spec],
        out_shape=[tab, tab, tab],
        compiler_params=_cparams("parallel"),
        name="rope_tables",
    )(pos_col, inv_row)


def _apply_rope(rp, cos, sina, sinb):
    return rp * cos + pltpu.roll(rp, 96, 1) * sina + pltpu.roll(rp, 32, 1) * sinb


IN_TN = 512
U_GATES = 0
U_QX = U_GATES + GATES_WIDTH
U_XBC = U_QX + QX_WIDTH
U_CKV = U_XBC + D_XBC
U_WIDTH = U_CKV + KV_LORA
_J_DT = (U_QX + QX_WIDTH - LANES) // IN_TN
assert (U_QX + QX_WIDTH) % IN_TN == 0 and U_WIDTH == W_IN_PERM_WIDTH


def _inproj_kernel(x_ref, nw_ref, w_ref, u_ref, dt_ref, h_ref):
    j = pl.program_id(1)

    @pl.when(j == 0)
    def _():
        h_ref[...] = _rms(x_ref[...], nw_ref[...]).astype(BF16)

    acc = jnp.dot(h_ref[...], w_ref[...], preferred_element_type=F32)
    u_ref[...] = acc.astype(BF16)

    @pl.when(j == _J_DT)
    def _():
        dt_ref[...] = acc[:, IN_TN - LANES:]


def _in_projection(x2, norm_w, w_perm, tm):
    n = x2.shape[0]
    grid = (n // tm, U_WIDTH // IN_TN)
    return pl.pallas_call(
        _inproj_kernel,
        grid=grid,
        in_specs=[
            pl.BlockSpec((tm, D_MODEL), lambda i, j: (i, 0)),
            pl.BlockSpec((1, D_MODEL), lambda i, j: (0, 0)),
            pl.BlockSpec((D_MODEL, IN_TN), lambda i, j: (0, j)),
        ],
        out_specs=[
            pl.BlockSpec((tm, IN_TN), lambda i, j: (i, j)),
            pl.BlockSpec((tm, LANES), lambda i, j: (i, 0)),
        ],
        out_shape=[
            jax.ShapeDtypeStruct((n, U_WIDTH), BF16),
            jax.ShapeDtypeStruct((n, LANES), F32),
        ],
        scratch_shapes=[pltpu.VMEM((tm, D_MODEL), BF16)],
        compiler_params=_cparams("parallel", "arbitrary"),
        name="in_projection",
    )(x2, norm_w, w_perm)


CONV_HALO = 16
CONV_SUB = 128
CONV_PAD = CONV_SUB // 2
CONV_SHIFTS = (-2, -1, 1)


def _conv_shift_matrix():
    t = jnp.arange(CONV_SUB)[:, None]
    r = jnp.arange(2 * CONV_SUB)[None, :]
    return jnp.concatenate([(r == CONV_PAD + t + sh) for sh in CONV_SHIFTS], axis=0).astype(BF16)


def _conv_kernel(prev_ref, main_ref, next_ref, sel_ref, w_ref, b_ref, out_ref, ext_ref, *, tm):
    i = pl.program_id(1)
    last = pl.num_programs(1) - 1
    lo = CONV_PAD - CONV_HALO
    hi = CONV_PAD + tm + CONV_HALO
    zeros = jnp.zeros((lo, ext_ref.shape[1]), BF16)
    ext_ref[0:lo, :] = zeros
    ext_ref[hi:hi + lo, :] = zeros
    ext_ref[lo:CONV_PAD, :] = jnp.where(i > 0, prev_ref[0], jnp.zeros_like(prev_ref[0]))
    ext_ref[CONV_PAD:CONV_PAD + tm, :] = main_ref[0]
    ext_ref[CONV_PAD + tm:hi, :] = jnp.where(i < last, next_ref[0], jnp.zeros_like(next_ref[0]))
    sel = sel_ref[...]
    for sb in range(tm // CONV_SUB):
        r0 = sb * CONV_SUB
        shifted = jnp.dot(sel, ext_ref[r0:r0 + 2 * CONV_SUB, :], preferred_element_type=F32)
        acc = b_ref[...] + w_ref[2:3, :] * main_ref[0, r0:r0 + CONV_SUB, :].astype(F32)
        for j, tap in enumerate((0, 1, 3)):
            acc = acc + w_ref[tap:tap + 1, :] * shifted[j * CONV_SUB:(j + 1) * CONV_SUB, :]
        out_ref[0, r0:r0 + CONV_SUB, :] = _silu(acc).astype(BF16)


def _conv_silu(u3, conv_w8, conv_b, tm, tc):
    b, s, _ = u3.shape
    hb = tm // CONV_HALO
    n_halo = s // CONV_HALO
    c0 = U_XBC // tc
    sel = _conv_shift_matrix()
    return pl.pallas_call(
        functools.partial(_conv_kernel, tm=tm),
        grid=(b, s // tm, D_XBC // tc),
        in_specs=[
            pl.BlockSpec((1, CONV_HALO, tc), lambda bi, i, ci: (bi, jnp.maximum(i * hb - 1, 0), c0 + ci)),
            pl.BlockSpec((1, tm, tc), lambda bi, i, ci: (bi, i, c0 + ci)),
            pl.BlockSpec((1, CONV_HALO, tc), lambda bi, i, ci: (bi, jnp.minimum((i + 1) * hb, n_halo - 1), c0 + ci)),
            pl.BlockSpec(sel.shape, lambda bi, i, ci: (0, 0)),
            pl.BlockSpec((8, tc), lambda bi, i, ci: (0, ci)),
            pl.BlockSpec((1, tc), lambda bi, i, ci: (0, ci)),
        ],
        out_specs=pl.BlockSpec((1, tm, tc), lambda bi, i, ci: (bi, i, ci)),
        out_shape=jax.ShapeDtypeStruct((b, s, D_XBC), BF16),
        scratch_shapes=[pltpu.VMEM((tm + 2 * CONV_PAD, tc), BF16)],
        compiler_params=_cparams("parallel", "parallel", "parallel"),
        name="conv_silu",
    )(u3, u3, u3, sel, conv_w8, conv_b)


def _ssd_chunk(direction, xc_ref, dt_ref, bias_ref, alog_ref, state_ref, y_ref):
    fwd = direction == 0
    xc = xc_ref[...]
    dtv = _softplus(dt_ref[...] + bias_ref[...])
    da = dtv * (-jnp.exp(alog_ref[...]))
    row = lax.broadcasted_iota(jnp.int32, (CHUNK, CHUNK), 0)
    col = lax.broadcasted_iota(jnp.int32, (CHUNK, CHUNK), 1)
    keep = (row >= col) if fwd else (row <= col)
    a_cs = jnp.dot(keep.astype(F32), da, precision=lax.Precision.HIGHEST, preferred_element_type=F32)
    a_cs_t = a_cs.T
    dt_t = dtv.T
    end = CHUNK - 1 if fwd else 0
    lane = lax.broadcasted_iota(jnp.int32, (1, LANES), 1)
    first_half = lane < SSM_HEAD_DIM

    for g in range(SSM_GROUPS):
        b_g = xc[:, D_SSM + g * D_STATE:D_SSM + (g + 1) * D_STATE]
        c_g = xc[:, D_SSM + SSM_GROUPS * D_STATE + g * D_STATE:D_SSM + SSM_GROUPS * D_STATE + (g + 1) * D_STATE]
        cb = lax.dot_general(c_g, b_g, (((1,), (1,)), ((), ())), preferred_element_type=F32)
        c_f = c_g.astype(F32)
        b_t = b_g.astype(F32).T
        for pair in range(HEADS_PER_GROUP // 2):
            p_idx = g * (HEADS_PER_GROUP // 2) + pair
            x_pair = xc[:, p_idx * LANES:(p_idx + 1) * LANES]
            s_old = state_ref[:, p_idx * LANES:(p_idx + 1) * LANES]
            rhs = jnp.concatenate([x_pair, s_old.astype(BF16)], axis=0)
            ys, news, decs = [], [], []
            for k in range(2):
                cidx = direction * N_SSM_HEADS + 2 * p_idx + k
                col_a = a_cs[:, cidx:cidx + 1]
                row_a = a_cs_t[cidx:cidx + 1, :]
                row_dt = dt_t[cidx:cidx + 1, :]
                decay = jnp.where(keep, jnp.exp(col_a - row_a), 0.0)
                m_h = (cb * decay * row_dt).astype(BF16)
                c_e = (c_f * jnp.exp(col_a)).astype(BF16)
                lhs = jnp.concatenate([m_h, c_e], axis=1)
                ys.append(jnp.dot(lhs, rhs, preferred_element_type=F32))
                tot = row_a[:, end:end + 1]
                w_row = jnp.exp(tot - row_a) * row_dt
                b_w = (b_t * w_row).astype(BF16)
                news.append(jnp.dot(b_w, x_pair, preferred_element_type=F32))
                decs.append(jnp.exp(tot))
            y_ref[:, p_idx * LANES:(p_idx + 1) * LANES] = jnp.where(first_half, ys[0], ys[1])
            dec = jnp.where(first_half, decs[0], decs[1])
            state_ref[:, p_idx * LANES:(p_idx + 1) * LANES] = s_old * dec + jnp.where(first_half, news[0], news[1])


def _ssd_fwd_kernel(xc_ref, dt_ref, bias_ref, alog_ref, yf_ref, state_ref, y_ref):
    @pl.when(pl.program_id(0) == 0)
    def _():
        state_ref[...] = jnp.zeros_like(state_ref)

    for b in range(xc_ref.shape[0]):
        _ssd_chunk(0, xc_ref.at[b], dt_ref.at[b], bias_ref, alog_ref, state_ref.at[b], y_ref.at[b])
        yf_ref[b] = y_ref[b].astype(BF16)


def _ssd_bwd_kernel(xc_ref, dt_ref, bias_ref, alog_ref, yf_ref, z_ref, dskip_ref, nw_ref, out_ref, state_ref, y_ref):
    @pl.when(pl.program_id(0) == 0)
    def _():
        state_ref[...] = jnp.zeros_like(state_ref)

    gw = D_SSM // SSM_GROUPS
    for b in range(xc_ref.shape[0]):
        _ssd_chunk(1, xc_ref.at[b], dt_ref.at[b], bias_ref, alog_ref, state_ref.at[b], y_ref.at[b])
        for g in range(SSM_GROUPS):
            sl = slice(g * gw, (g + 1) * gw)
            xs = xc_ref[b, :, sl].astype(F32)
            y = yf_ref[b, :, sl].astype(F32) + y_ref[b, :, sl] + dskip_ref[:, sl] * xs
            y = y * _silu(z_ref[b, :, sl].astype(F32))
            out_ref[b, :, sl] = _rms(y, nw_ref[:, sl]).astype(BF16)


def _ssd(xc3, dt3, bias_row, alog_row, u3, d_skip_row, ssm_norm_w):
    batch, seq, _ = xc3.shape
    nc = seq // CHUNK
    row_spec = lambda width: pl.BlockSpec((batch, CHUNK, width), lambda c: (0, c, 0))
    rev_spec = lambda width: pl.BlockSpec((batch, CHUNK, width), lambda c: (0, nc - 1 - c, 0))
    const = lambda width: pl.BlockSpec((1, width), lambda c: (0, 0))
    scratch = [pltpu.VMEM((batch, D_STATE, D_SSM), F32), pltpu.VMEM((batch, CHUNK, D_SSM), F32)]
    out_shape = jax.ShapeDtypeStruct((batch, seq, D_SSM), BF16)
    y_fwd = pl.pallas_call(
        _ssd_fwd_kernel,
        grid=(nc,),
        in_specs=[row_spec(D_XBC), row_spec(LANES), const(LANES), const(LANES)],
        out_specs=row_spec(D_SSM),
        out_shape=out_shape,
        scratch_shapes=scratch,
        compiler_params=_cparams("arbitrary"),
        name="ssd_forward",
    )(xc3, dt3, bias_row, alog_row)
    return pl.pallas_call(
        _ssd_bwd_kernel,
        grid=(nc,),
        in_specs=[rev_spec(D_XBC), rev_spec(LANES), const(LANES), const(LANES), rev_spec(D_SSM),
                  rev_spec(D_SSM), const(D_SSM), const(D_SSM)],
        out_specs=rev_spec(D_SSM),
        out_shape=out_shape,
        scratch_shapes=scratch,
        compiler_params=_cparams("arbitrary"),
        name="ssd_backward",
    )(xc3, dt3, bias_row, alog_row, y_fwd, u3, d_skip_row, ssm_norm_w)


Q_SCALE = (1.0 / math.sqrt(QK_NOPE_DIM + QK_ROPE_DIM)) * math.log2(math.e)


def _qproj_kernel(qx_ref, nw_ref, w_ref, cos_ref, sina_ref, sinb_ref, q_ref):
    cq = _rms(qx_ref[:, :Q_LORA].astype(F32), nw_ref[...]).astype(BF16)
    cos, sina, sinb = cos_ref[...], sina_ref[...], sinb_ref[...]
    for h in range(N_ATTN_HEADS):
        a = jnp.dot(cq, w_ref[:, h * QK_PAD_DIM:(h + 1) * QK_PAD_DIM], preferred_element_type=F32)
        q_ref[0, h, :, :LANES] = (a[:, :LANES] * Q_SCALE).astype(BF16)
        q_ref[0, h, :, LANES:] = (_apply_rope(a[:, LANES:], cos, sina, sinb) * Q_SCALE).astype(BF16)


def _q_projection(u, q_norm_w, w_q, cos, sina, sinb, batch, seq, tm):
    nq = seq // tm
    tab = pl.BlockSpec((tm, LANES), lambda i: (i, 0))
    return pl.pallas_call(
        _qproj_kernel,
        grid=(batch * nq,),
        in_specs=[
            pl.BlockSpec((tm, QX_WIDTH), lambda i: (i, U_QX // QX_WIDTH)),
            pl.BlockSpec((1, Q_LORA), lambda i: (0, 0)),
            pl.BlockSpec((Q_LORA, N_ATTN_HEADS * QK_PAD_DIM), lambda i: (0, 0)),
            tab, tab, tab,
        ],
        out_specs=pl.BlockSpec((1, N_ATTN_HEADS, tm, QK_PAD_DIM), lambda i: (i // nq, 0, i % nq, 0)),
        out_shape=jax.ShapeDtypeStruct((batch, N_ATTN_HEADS, seq, QK_PAD_DIM), BF16),
        compiler_params=_cparams("parallel"),
        name="q_projection",
    )(u, q_norm_w, w_q, cos, sina, sinb)


ATTN_TK = 512
ATTN_SLOTS = 2
KV_HEAD_GROUP = 4


def _kvproj_kernel(ckv_ref, kr_ref, nw_ref, wkt_ref, wv_ref, cos_ref, sina_ref, sinb_ref, kt_ref, v_ref):
    c = _rms(ckv_ref[...].astype(F32), nw_ref[...]).astype(BF16)
    kr = _apply_rope(kr_ref[...].astype(F32), cos_ref[...], sina_ref[...], sinb_ref[...])
    kr_t = kr.T.astype(BF16)
    nt = (((1,), (1,)), ((), ()))
    cols = KV_HEAD_GROUP * QK_NOPE_DIM
    for g in range(N_ATTN_HEADS // KV_HEAD_GROUP):
        kn_t = lax.dot_general(wkt_ref[g * cols:(g + 1) * cols, :], c, nt, preferred_element_type=F32).astype(BF16)
        v = jnp.dot(c, wv_ref[:, g * cols:(g + 1) * cols], preferred_element_type=F32).astype(BF16)
        for k in range(KV_HEAD_GROUP):
            h = g * KV_HEAD_GROUP + k
            kt_ref[0, h, 0, :QK_NOPE_DIM, :] = kn_t[k * QK_NOPE_DIM:(k + 1) * QK_NOPE_DIM, :]
            kt_ref[0, h, 0, QK_NOPE_DIM:, :] = kr_t
            v_ref[0, h] = v[:, k * V_HEAD_DIM:(k + 1) * V_HEAD_DIM]


def _kv_projection(u, kv_norm_w, w_kt, w_v, cos, sina, sinb, batch, seq):
    tk = ATTN_TK
    nk = seq // tk
    tab = pl.BlockSpec((tk, LANES), lambda i: (i, 0))
    return pl.pallas_call(
        _kvproj_kernel,
        grid=(batch * nk,),
        in_specs=[
            pl.BlockSpec((tk, KV_LORA), lambda i: (i, U_CKV // KV_LORA)),
            pl.BlockSpec((tk, LANES), lambda i: (i, (U_QX + Q_LORA) // LANES)),
            pl.BlockSpec((1, KV_LORA), lambda i: (0, 0)),
            pl.BlockSpec((N_ATTN_HEADS * QK_NOPE_DIM, KV_LORA), lambda i: (0, 0)),
            pl.BlockSpec((KV_LORA, N_ATTN_HEADS * V_HEAD_DIM), lambda i: (0, 0)),
            tab, tab, tab,
        ],
        out_specs=[
            pl.BlockSpec((1, N_ATTN_HEADS, 1, QK_PAD_DIM, tk), lambda i: (i // nk, 0, i % nk, 0, 0)),
            pl.BlockSpec((1, N_ATTN_HEADS, tk, V_HEAD_DIM), lambda i: (i // nk, 0, i % nk, 0)),
        ],
        out_shape=[
            jax.ShapeDtypeStruct((batch, N_ATTN_HEADS, nk, QK_PAD_DIM, tk), BF16),
            jax.ShapeDtypeStruct((batch, N_ATTN_HEADS, seq, V_HEAD_DIM), BF16),
        ],
        compiler_params=_cparams("parallel"),
        name="kv_projection",
    )(u, u, kv_norm_w, w_kt, w_v, cos, sina, sinb)


def _attn_kernel(q_ref, kt_ref, v_ref, z_ref, o_ref, s_ref, p_ref, a_ref, m_ref, l_ref, acc_ref, *, tk, nk, unroll):
    period = s_ref.shape[0]
    reps = tk // LANES
    m_ref[...] = jnp.full_like(m_ref, -jnp.inf)
    l_ref[...] = jnp.zeros_like(l_ref)
    acc_ref[...] = jnp.zeros_like(acc_ref)

    def scores(j, slot):
        s_ref[slot] = jnp.dot(q_ref[0, 0], kt_ref[0, 0, j], preferred_element_type=F32)

    def softmax(slot):
        s = s_ref[slot]
        m_prev = m_ref[...]
        m_new = jnp.maximum(m_prev, jnp.max(s, axis=1, keepdims=True))
        alpha = jnp.exp2(m_prev - m_new)
        p = jnp.exp2(s - jnp.concatenate([m_new] * reps, axis=1))
        l_part = p[:, :LANES]
        for r in range(1, reps):
            l_part = l_part + p[:, r * LANES:(r + 1) * LANES]
        l_ref[...] = alpha * l_ref[...] + l_part
        m_ref[...] = m_new
        a_ref[slot] = alpha
        p_ref[slot] = p.astype(BF16)

    def values(j, slot):
        start = pl.multiple_of(j * tk, tk)
        pv = jnp.dot(p_ref[slot], v_ref[0, 0, pl.ds(start, tk), :], preferred_element_type=F32)
        acc_ref[...] = a_ref[slot] * acc_ref[...] + pv

    def step(j, jm):
        traced = not isinstance(j, int)
        if traced or j >= 1:
            values(j - 1, (jm - 1) % period)
        if traced or j + 1 < nk:
            scores(j + 1, (jm + 1) % period)
        softmax(jm % period)

    assert unroll % period == 0, "buffer slots must be static in the loop body"
    scores(0, 0)
    step(0, 0)
    n_loop = max(nk - 2, 0) // unroll

    def group(i, carry):
        for u in range(unroll):
            step(1 + unroll * i + u, (1 + u) % period)
        return carry

    lax.fori_loop(0, n_loop, group, 0)
    for j in range(1 + unroll * n_loop, nk):
        step(j, j % period)
    values(nk - 1, (nk - 1) % period)
    l = jnp.sum(l_ref[...], axis=1, keepdims=True)
    o_ref[...] = (acc_ref[...] / l * _silu(z_ref[...].astype(F32))).astype(BF16)


def _attention(q, kt, v, u, batch, seq, tq, unroll):
    tk = ATTN_TK
    nk = seq // tk
    nq = seq // tq
    z_col0 = (U_GATES + D_MODEL) // V_HEAD_DIM
    stat = pltpu.VMEM((tq, LANES), F32)
    return pl.pallas_call(
        functools.partial(_attn_kernel, tk=tk, nk=nk, unroll=unroll),
        grid=(batch, N_ATTN_HEADS, nq),
        in_specs=[
            pl.BlockSpec((1, 1, tq, QK_PAD_DIM), lambda b, h, i: (b, h, i, 0)),
            pl.BlockSpec((1, 1, nk, QK_PAD_DIM, tk), lambda b, h, i: (b, h, 0, 0, 0)),
            pl.BlockSpec((1, 1, seq, V_HEAD_DIM), lambda b, h, i: (b, h, 0, 0)),
            pl.BlockSpec((tq, V_HEAD_DIM), lambda b, h, i: (b * nq + i, z_col0 + h)),
        ],
        out_specs=pl.BlockSpec((tq, V_HEAD_DIM), lambda b, h, i: (b * nq + i, h)),
        out_shape=jax.ShapeDtypeStruct((batch * seq, D_ATTN), BF16),
        scratch_shapes=[pltpu.VMEM((ATTN_SLOTS, tq, tk), F32), pltpu.VMEM((ATTN_SLOTS, tq, tk), BF16),
                        pltpu.VMEM((ATTN_SLOTS, tq, LANES), F32), stat, stat, pltpu.VMEM((tq, V_HEAD_DIM), F32)],
        compiler_params=_cparams("parallel", "parallel", "arbitrary"),
        name="attention",
    )(q, kt, v, u)


def _merge_kernel(y_ref, o_ref, gs_ref, ga_ref, wps_ref, wpa_ref, out_ref):
    a = jnp.dot(y_ref[...], wps_ref[...], preferred_element_type=F32)
    merged = _sigmoid(gs_ref[...].astype(F32)) * a
    b = jnp.dot(o_ref[...], wpa_ref[...], preferred_element_type=F32)
    merged = merged + _sigmoid(ga_ref[...].astype(F32)) * b
    out_ref[...] = merged.astype(BF16)


def _merge(y, o, gates, w_ps, w_pa, tm):
    n = y.shape[0]
    row = pl.BlockSpec((tm, D_MODEL), lambda i: (i, 0))
    wspec = pl.BlockSpec((D_MODEL, D_MODEL), lambda i: (0, 0))
    return pl.pallas_call(
        _merge_kernel,
        grid=(n // tm,),
        in_specs=[row, row,
                  pl.BlockSpec((tm, D_MODEL), lambda i: (i, 2)),
                  pl.BlockSpec((tm, D_MODEL), lambda i: (i, 3)),
                  wspec, wspec],
        out_specs=row,
        out_shape=jax.ShapeDtypeStruct((n, D_MODEL), BF16),
        compiler_params=_cparams("parallel"),
        name="gated_merge",
    )(y, o, gates, gates, w_ps, w_pa)


def _merge_out_kernel(y_ref, o_ref, gs_ref, ga_ref, x_ref, wps_ref, wpa_ref, wout_ref, nw_ref, out_ref):
    a = jnp.dot(y_ref[...], wps_ref[...], preferred_element_type=F32)
    merged = _sigmoid(gs_ref[...].astype(F32)) * a
    b = jnp.dot(o_ref[...], wpa_ref[...], preferred_element_type=F32)
    merged = merged + _sigmoid(ga_ref[...].astype(F32)) * b
    r = x_ref[...] + jnp.dot(merged.astype(BF16), wout_ref[...], preferred_element_type=F32)
    out_ref[...] = _rms(r, nw_ref[...])


def _merge_out(y, o, u, x2, w_ps, w_pa, w_out, final_w, tm):
    n = y.shape[0]
    row = pl.BlockSpec((tm, D_MODEL), lambda i: (i, 0))
    wspec = pl.BlockSpec((D_MODEL, D_MODEL), lambda i: (0, 0), pipeline_mode=pl.Buffered(1))
    return pl.pallas_call(
        _merge_out_kernel,
        grid=(n // tm,),
        in_specs=[row, row,
                  pl.BlockSpec((tm, D_MODEL), lambda i: (i, 2)),
                  pl.BlockSpec((tm, D_MODEL), lambda i: (i, 3)),
                  row, wspec, wspec, wspec, pl.BlockSpec((1, D_MODEL), lambda i: (0, 0))],
        out_specs=row,
        out_shape=jax.ShapeDtypeStruct((n, D_MODEL), F32),
        compiler_params=_cparams("parallel"),
        name="merge_out_projection",
    )(y, o, u, u, x2, w_ps, w_pa, w_out, final_w)


def _out_kernel(m_ref, x_ref, w_ref, nw_ref, out_ref):
    r = x_ref[...] + jnp.dot(m_ref[...], w_ref[...], preferred_element_type=F32)
    out_ref[...] = _rms(r, nw_ref[...])


def _out_projection(merged, x2, w_out, final_w, tm):
    n = x2.shape[0]
    row = pl.BlockSpec((tm, D_MODEL), lambda i: (i, 0))
    return pl.pallas_call(
        _out_kernel,
        grid=(n // tm,),
        in_specs=[row, row, pl.BlockSpec((D_MODEL, D_MODEL), lambda i: (0, 0)),
                  pl.BlockSpec((1, D_MODEL), lambda i: (0, 0))],
        out_specs=row,
        out_shape=jax.ShapeDtypeStruct((n, D_MODEL), F32),
        compiler_params=_cparams("parallel"),
        name="out_projection",
    )(merged, x2, w_out, final_w)


def _permute_w_in(w_in):
    sizes = (D_SSM, D_XBC, N_SSM_HEADS, N_SSM_HEADS, Q_LORA, KV_LORA, QK_ROPE_DIM, D_ATTN, D_MODEL, D_MODEL)
    offs = [0]
    for sz in sizes:
        offs.append(offs[-1] + sz)
    w = w_in.astype(BF16)
    z_ssm, xbc, dt_f, dt_b, q_a, c_kv, k_r, z_attn, g_ssm, g_attn = (
        w[:, offs[k]:offs[k + 1]] for k in range(len(sizes)))
    pad64 = jnp.zeros((D_MODEL, LANES - QK_ROPE_DIM), BF16)
    cols = [z_ssm, z_attn, g_ssm, g_attn, q_a, k_r, pad64, dt_f, dt_b, pad64, xbc, c_kv]
    return jnp.concatenate(cols, axis=1)


def _permute_w_uq(w_uq):
    w = w_uq.astype(BF16).reshape(Q_LORA, N_ATTN_HEADS, QK_NOPE_DIM + QK_ROPE_DIM)
    pad = jnp.zeros((Q_LORA, N_ATTN_HEADS, QK_PAD_DIM - QK_NOPE_DIM - QK_ROPE_DIM), BF16)
    return jnp.concatenate([w, pad], axis=2).reshape(Q_LORA, N_ATTN_HEADS * QK_PAD_DIM)


def _layer(x2, tabs, batch, seq, norm_w, w_in, conv_w, conv_b, a_log_fwd, a_log_bwd, dt_bias_fwd, dt_bias_bwd,
           d_skip, ssm_norm_w, q_norm_w, w_uq, kv_norm_w, w_ukv, w_proj_ssm, w_proj_attn, w_out):
    cos, sina, sinb = tabs
    row = lambda v: v.reshape(1, -1).astype(F32)
    pad_row = lambda f, b: jnp.concatenate([f, b, jnp.zeros((LANES - 2 * N_SSM_HEADS,), F32)]).reshape(1, LANES)

    u, dt = _in_projection(x2, row(norm_w), _permute_w_in(w_in), tm=1024)

    conv_w8 = jnp.concatenate([conv_w.astype(F32), jnp.zeros((4, D_XBC), F32)], axis=0)
    u3 = u.reshape(batch, seq, U_WIDTH)
    xc = _conv_silu(u3, conv_w8, row(conv_b), tm=512, tc=512)
    y = _ssd(xc, dt.reshape(batch, seq, LANES), pad_row(dt_bias_fwd, dt_bias_bwd), pad_row(a_log_fwd, a_log_bwd),
             u3, row(jnp.repeat(d_skip, SSM_HEAD_DIM)), row(ssm_norm_w)).reshape(batch * seq, D_SSM)

    w_kv = w_ukv.astype(BF16).reshape(KV_LORA, N_ATTN_HEADS, QK_NOPE_DIM + V_HEAD_DIM)
    w_kt = w_kv[:, :, :QK_NOPE_DIM].reshape(KV_LORA, N_ATTN_HEADS * QK_NOPE_DIM).T
    w_v = w_kv[:, :, QK_NOPE_DIM:].reshape(KV_LORA, N_ATTN_HEADS * V_HEAD_DIM)
    q = _q_projection(u, row(q_norm_w), _permute_w_uq(w_uq), cos, sina, sinb, batch, seq, tm=512)
    kt, v = _kv_projection(u, row(kv_norm_w), w_kt, w_v, cos, sina, sinb, batch, seq)
    o = _attention(q, kt, v, u, batch, seq, tq=512, unroll=14)

    return y, o, u


def kernel(x, positions, norm_w, w_in, conv_w, conv_b, a_log_fwd, a_log_bwd, dt_bias_fwd, dt_bias_bwd, d_skip,
           ssm_norm_w, q_norm_w, w_uq, kv_norm_w, w_ukv, w_proj_ssm, w_proj_attn, w_out, final_norm_w):
    batch, seq, _ = x.shape
    depth = norm_w.shape[0]
    assert depth == 1, "the single output projection is fused with the final norm"
    inv = 1.0 / (ROPE_THETA ** (jnp.arange(0, QK_ROPE_DIM, 2, dtype=F32) / QK_ROPE_DIM))
    inv_row = jnp.concatenate([inv, inv, jnp.zeros((LANES - QK_ROPE_DIM,), F32)]).reshape(1, LANES)
    tabs = _rope_tables(positions.reshape(batch * seq, 1), inv_row, tm=1024)
    x2 = x.reshape(batch * seq, D_MODEL)
    y, o, u = _layer(x2, tabs, batch, seq, norm_w[0], w_in[0], conv_w[0], conv_b[0], a_log_fwd[0], a_log_bwd[0],
                     dt_bias_fwd[0], dt_bias_bwd[0], d_skip[0], ssm_norm_w[0], q_norm_w[0], w_uq[0],
                     kv_norm_w[0], w_ukv[0], w_proj_ssm[0], w_proj_attn[0], w_out[0])
    out = _merge_out(y, o, u, x2, w_proj_ssm[0].astype(BF16), w_proj_attn[0].astype(BF16), w_out[0].astype(BF16),
                     final_norm_w.reshape(1, -1).astype(F32), tm=256)
    return out.reshape(batch, seq, D_MODEL)
```

```python
import functools
import math

import jax
import jax.numpy as jnp
from jax import lax
from jax.experimental import pallas as pl
from jax.experimental.pallas import tpu as pltpu

F32 = jnp.float32
BF16 = jnp.bfloat16

D_MODEL = 2048
D_SSM = 2048
SSM_HEAD_DIM = 64
N_SSM_HEADS = 32
SSM_GROUPS = 4
HEADS_PER_GROUP = N_SSM_HEADS // SSM_GROUPS
D_STATE = 128
CHUNK = 128
D_XBC = D_SSM + 2 * SSM_GROUPS * D_STATE
N_ATTN_HEADS = 16
QK_NOPE_DIM = 128
QK_ROPE_DIM = 64
V_HEAD_DIM = 128
D_ATTN = N_ATTN_HEADS * V_HEAD_DIM
Q_LORA = 768
KV_LORA = 512
ROPE_THETA = 10000.0
EPS = 1e-6

LANES = 128
QK_PAD_DIM = 2 * LANES
QX_WIDTH = 1024
GATES_WIDTH = 4 * D_MODEL
W_IN_PERM_WIDTH = GATES_WIDTH + D_XBC + KV_LORA + QX_WIDTH

VMEM_LIMIT = 56 * 1024 * 1024


def _cparams(*sem):
    return pltpu.CompilerParams(dimension_semantics=sem, vmem_limit_bytes=VMEM_LIMIT)


def _sigmoid(x):
    return 1.0 / (1.0 + jnp.exp(-x))


def _silu(x):
    return x * _sigmoid(x)


def _softplus(x):
    return jnp.maximum(x, 0.0) + jnp.log1p(jnp.exp(-jnp.abs(x)))


def _rms(x, w):
    ms = jnp.mean(x * x, axis=-1, keepdims=True)
    return x * lax.rsqrt(ms + EPS) * w


def _rope_kernel(pos_ref, inv_ref, cos_ref, sina_ref, sinb_ref):
    ang = pos_ref[...].astype(F32) * inv_ref[...]
    lane = lax.broadcasted_iota(jnp.int32, ang.shape, 1)
    c = jnp.cos(ang)
    s = jnp.sin(ang)
    cos_ref[...] = jnp.where(lane < 64, c, 0.0)
    sina_ref[...] = jnp.where(lane < 32, -s, 0.0)
    sinb_ref[...] = jnp.where((lane >= 32) & (lane < 64), s, 0.0)


def _rope_tables(pos_col, inv_row, tm):
    n = pos_col.shape[0]
    tab = jax.ShapeDtypeStruct((n, LANES), F32)
    spec = pl.BlockSpec((tm, LANES), lambda i: (i, 0))
    return pl.pallas_call(
        _rope_kernel,
        grid=(n // tm,),
        in_specs=[pl.BlockSpec((tm, 1), lambda i: (i, 0)), pl.BlockSpec((1, LANES), lambda i: (0, 0))],
        out_specs=[spec, spec, spec],
        out_shape=[tab, tab, tab],
        compiler_params=_cparams("parallel"),
        name="rope_tables",
    )(pos_col, inv_row)


def _apply_rope(rp, cos, sina, sinb):
    return rp * cos + pltpu.roll(rp, 96, 1) * sina + pltpu.roll(rp, 32, 1) * sinb


IN_TN = 512
U_GATES = 0
U_QX = U_GATES + GATES_WIDTH
U_XBC = U_QX + QX_WIDTH
U_CKV = U_XBC + D_XBC
U_WIDTH = U_CKV + KV_LORA
_J_DT = (U_QX + QX_WIDTH - LANES) // IN_TN
assert (U_QX + QX_WIDTH) % IN_TN == 0 and U_WIDTH == W_IN_PERM_WIDTH


def _inproj_kernel(x_ref, nw_ref, w_ref, u_ref, dt_ref, h_ref):
    j = pl.program_id(1)

    @pl.when(j == 0)
    def _():
        h_ref[...] = _rms(x_ref[...], nw_ref[...]).astype(BF16)

    acc = jnp.dot(h_ref[...], w_ref[...], preferred_element_type=F32)
    u_ref[...] = acc.astype(BF16)

    @pl.when(j == _J_DT)
    def _():
        dt_ref[...] = acc[:, IN_TN - LANES:]


def _in_projection(x2, norm_w, w_perm, tm):
    n = x2.shape[0]
    grid = (n // tm, U_WIDTH // IN_TN)
    return pl.pallas_call(
        _inproj_kernel,
        grid=grid,
        in_specs=[
            pl.BlockSpec((tm, D_MODEL), lambda i, j: (i, 0)),
            pl.BlockSpec((1, D_MODEL), lambda i, j: (0, 0)),
            pl.BlockSpec((D_MODEL, IN_TN), lambda i, j: (0, j)),
        ],
        out_specs=[
            pl.BlockSpec((tm, IN_TN), lambda i, j: (i, j)),
            pl.BlockSpec((tm, LANES), lambda i, j: (i, 0)),
        ],
        out_shape=[
            jax.ShapeDtypeStruct((n, U_WIDTH), BF16),
            jax.ShapeDtypeStruct((n, LANES), F32),
        ],
        scratch_shapes=[pltpu.VMEM((tm, D_MODEL), BF16)],
        compiler_params=_cparams("parallel", "arbitrary"),
        name="in_projection",
    )(x2, norm_w, w_perm)


CONV_HALO = 16
CONV_SUB = 128
CONV_PAD = CONV_SUB // 2
CONV_SHIFTS = (-2, -1, 1)


def _conv_shift_matrix():
    t = jnp.arange(CONV_SUB)[:, None]
    r = jnp.arange(2 * CONV_SUB)[None, :]
    return jnp.concatenate([(r == CONV_PAD + t + sh) for sh in CONV_SHIFTS], axis=0).astype(BF16)


def _conv_kernel(prev_ref, main_ref, next_ref, sel_ref, w_ref, b_ref, out_ref, ext_ref, *, tm):
    i = pl.program_id(1)
    last = pl.num_programs(1) - 1
    lo = CONV_PAD - CONV_HALO
    hi = CONV_PAD + tm + CONV_HALO
    zeros = jnp.zeros((lo, ext_ref.shape[1]), BF16)
    ext_ref[0:lo, :] = zeros
    ext_ref[hi:hi + lo, :] = zeros
    ext_ref[lo:CONV_PAD, :] = jnp.where(i > 0, prev_ref[0], jnp.zeros_like(prev_ref[0]))
    ext_ref[CONV_PAD:CONV_PAD + tm, :] = main_ref[0]
    ext_ref[CONV_PAD + tm:hi, :] = jnp.where(i < last, next_ref[0], jnp.zeros_like(next_ref[0]))
    sel = sel_ref[...]
    for sb in range(tm // CONV_SUB):
        r0 = sb * CONV_SUB
        shifted = jnp.dot(sel, ext_ref[r0:r0 + 2 * CONV_SUB, :], preferred_element_type=F32)
        acc = b_ref[...] + w_ref[2:3, :] * main_ref[0, r0:r0 + CONV_SUB, :].astype(F32)
        for j, tap in enumerate((0, 1, 3)):
            acc = acc + w_ref[tap:tap + 1, :] * shifted[j * CONV_SUB:(j + 1) * CONV_SUB, :]
        out_ref[0, r0:r0 + CONV_SUB, :] = _silu(acc).astype(BF16)


def _conv_silu(u3, conv_w8, conv_b, tm, tc):
    b, s, _ = u3.shape
    hb = tm // CONV_HALO
    n_halo = s // CONV_HALO
    c0 = U_XBC // tc
    sel = _conv_shift_matrix()
    return pl.pallas_call(
        functools.partial(_conv_kernel, tm=tm),
        grid=(b, s // tm, D_XBC // tc),
        in_specs=[
            pl.BlockSpec((1, CONV_HALO, tc), lambda bi, i, ci: (bi, jnp.maximum(i * hb - 1, 0), c0 + ci)),
            pl.BlockSpec((1, tm, tc), lambda bi, i, ci: (bi, i, c0 + ci)),
            pl.BlockSpec((1, CONV_HALO, tc), lambda bi, i, ci: (bi, jnp.minimum((i + 1) * hb, n_halo - 1), c0 + ci)),
            pl.BlockSpec(sel.shape, lambda bi, i, ci: (0, 0)),
            pl.BlockSpec((8, tc), lambda bi, i, ci: (0, ci)),
            pl.BlockSpec((1, tc), lambda bi, i, ci: (0, ci)),
        ],
        out_specs=pl.BlockSpec((1, tm, tc), lambda bi, i, ci: (bi, i, ci)),
        out_shape=jax.ShapeDtypeStruct((b, s, D_XBC), BF16),
        scratch_shapes=[pltpu.VMEM((tm + 2 * CONV_PAD, tc), BF16)],
        compiler_params=_cparams("parallel", "parallel", "parallel"),
        name="conv_silu",
    )(u3, u3, u3, sel, conv_w8, conv_b)


def _ssd_chunk(direction, xc_ref, dt_ref, bias_ref, alog_ref, state_ref, y_ref):
    fwd = direction == 0
    xc = xc_ref[...]
    dtv = _softplus(dt_ref[...] + bias_ref[...])
    da = dtv * (-jnp.exp(alog_ref[...]))
    row = lax.broadcasted_iota(jnp.int32, (CHUNK, CHUNK), 0)
    col = lax.broadcasted_iota(jnp.int32, (CHUNK, CHUNK), 1)
    keep = (row >= col) if fwd else (row <= col)
    a_cs = jnp.dot(keep.astype(F32), da, precision=lax.Precision.HIGHEST, preferred_element_type=F32)
    a_cs_t = a_cs.T
    dt_t = dtv.T
    end = CHUNK - 1 if fwd else 0
    lane = lax.broadcasted_iota(jnp.int32, (1, LANES), 1)
    first_half = lane < SSM_HEAD_DIM

    for g in range(SSM_GROUPS):
        b_g = xc[:, D_SSM + g * D_STATE:D_SSM + (g + 1) * D_STATE]
        c_g = xc[:, D_SSM + SSM_GROUPS * D_STATE + g * D_STATE:D_SSM + SSM_GROUPS * D_STATE + (g + 1) * D_STATE]
        cb = lax.dot_general(c_g, b_g, (((1,), (1,)), ((), ())), preferred_element_type=F32)
        c_f = c_g.astype(F32)
        b_t = b_g.astype(F32).T
        for pair in range(HEADS_PER_GROUP // 2):
            p_idx = g * (HEADS_PER_GROUP // 2) + pair
            x_pair = xc[:, p_idx * LANES:(p_idx + 1) * LANES]
            s_old = state_ref[:, p_idx * LANES:(p_idx + 1) * LANES]
            rhs = jnp.concatenate([x_pair, s_old.astype(BF16)], axis=0)
            ys, news, decs = [], [], []
            for k in range(2):
                cidx = direction * N_SSM_HEADS + 2 * p_idx + k
                col_a = a_cs[:, cidx:cidx + 1]
                row_a = a_cs_t[cidx:cidx + 1, :]
                row_dt = dt_t[cidx:cidx + 1, :]
                decay = jnp.where(keep, jnp.exp(col_a - row_a), 0.0)
                m_h = (cb * decay * row_dt).astype(BF16)
                c_e = (c_f * jnp.exp(col_a)).astype(BF16)
                lhs = jnp.concatenate([m_h, c_e], axis=1)
                ys.append(jnp.dot(lhs, rhs, preferred_element_type=F32))
                tot = row_a[:, end:end + 1]
                w_row = jnp.exp(tot - row_a) * row_dt
                b_w = (b_t * w_row).astype(BF16)
                news.append(jnp.dot(b_w, x_pair, preferred_element_type=F32))
                decs.append(jnp.exp(tot))
            y_ref[:, p_idx * LANES:(p_idx + 1) * LANES] = jnp.where(first_half, ys[0], ys[1])
            dec = jnp.where(first_half, decs[0], decs[1])
            state_ref[:, p_idx * LANES:(p_idx + 1) * LANES] = s_old * dec + jnp.where(first_half, news[0], news[1])


def _ssd_fwd_kernel(xc_ref, dt_ref, bias_ref, alog_ref, yf_ref, state_ref, y_ref):
    @pl.when(pl.program_id(0) == 0)
    def _():
        state_ref[...] = jnp.zeros_like(state_ref)

    for b in range(xc_ref.shape[0]):
        _ssd_chunk(0, xc_ref.at[b], dt_ref.at[b], bias_ref, alog_ref, state_ref.at[b], y_ref.at[b])
        yf_ref[b] = y_ref[b].astype(BF16)


def _ssd_bwd_kernel(xc_ref, dt_ref, bias_ref, alog_ref, yf_ref, z_ref, dskip_ref, nw_ref, out_ref, state_ref, y_ref):
    @pl.when(pl.program_id(0) == 0)
    def _():
        state_ref[...] = jnp.zeros_like(state_ref)

    gw = D_SSM // SSM_GROUPS
    for b in range(xc_ref.shape[0]):
        _ssd_chunk(1, xc_ref.at[b], dt_ref.at[b], bias_ref, alog_ref, state_ref.at[b], y_ref.at[b])
        for g in range(SSM_GROUPS):
            sl = slice(g * gw, (g + 1) * gw)
            xs = xc_ref[b, :, sl].astype(F32)
            y = yf_ref[b, :, sl].astype(F32) + y_ref[b, :, sl] + dskip_ref[:, sl] * xs
            y = y * _silu(z_ref[b, :, sl].astype(F32))
            out_ref[b, :, sl] = _rms(y, nw_ref[:, sl]).astype(BF16)


def _ssd(xc3, dt3, bias_row, alog_row, u3, d_skip_row, ssm_norm_w):
    batch, seq, _ = xc3.shape
    nc = seq // CHUNK
    row_spec = lambda width: pl.BlockSpec((batch, CHUNK, width), lambda c: (0, c, 0))
    rev_spec = lambda width: pl.BlockSpec((batch, CHUNK, width), lambda c: (0, nc - 1 - c, 0))
    const = lambda width: pl.BlockSpec((1, width), lambda c: (0, 0))
    scratch = [pltpu.VMEM((batch, D_STATE, D_SSM), F32), pltpu.VMEM((batch, CHUNK, D_SSM), F32)]
    out_shape = jax.ShapeDtypeStruct((batch, seq, D_SSM), BF16)
    y_fwd = pl.pallas_call(
        _ssd_fwd_kernel,
        grid=(nc,),
        in_specs=[row_spec(D_XBC), row_spec(LANES), const(LANES), const(LANES)],
        out_specs=row_spec(D_SSM),
        out_shape=out_shape,
        scratch_shapes=scratch,
        compiler_params=_cparams("arbitrary"),
        name="ssd_forward",
    )(xc3, dt3, bias_row, alog_row)
    return pl.pallas_call(
        _ssd_bwd_kernel,
        grid=(nc,),
        in_specs=[rev_spec(D_XBC), rev_spec(LANES), const(LANES), const(LANES), rev_spec(D_SSM),
                  rev_spec(D_SSM), const(D_SSM), const(D_SSM)],
        out_specs=rev_spec(D_SSM),
        out_shape=out_shape,
        scratch_shapes=scratch,
        compiler_params=_cparams("arbitrary"),
        name="ssd_backward",
    )(xc3, dt3, bias_row, alog_row, y_fwd, u3, d_skip_row, ssm_norm_w)


Q_SCALE = (1.0 / math.sqrt(QK_NOPE_DIM + QK_ROPE_DIM)) * math.log2(math.e)


def _qproj_kernel(qx_ref, nw_ref, w_ref, cos_ref, sina_ref, sinb_ref, q_ref):
    cq = _rms(qx_ref[:, :Q_LORA].astype(F32), nw_ref[...]).astype(BF16)
    cos, sina, sinb = cos_ref[...], sina_ref[...], sinb_ref[...]
    for h in range(N_ATTN_HEADS):
        a = jnp.dot(cq, w_ref[:, h * QK_PAD_DIM:(h + 1) * QK_PAD_DIM], preferred_element_type=F32)
        q_ref[0, h, :, :LANES] = (a[:, :LANES] * Q_SCALE).astype(BF16)
        q_ref[0, h, :, LANES:] = (_apply_rope(a[:, LANES:], cos, sina, sinb) * Q_SCALE).astype(BF16)


def _q_projection(u, q_norm_w, w_q, cos, sina, sinb, batch, seq, tm):
    nq = seq // tm
    tab = pl.BlockSpec((tm, LANES), lambda i: (i, 0))
    return pl.pallas_call(
        _qproj_kernel,
        grid=(batch * nq,),
        in_specs=[
            pl.BlockSpec((tm, QX_WIDTH), lambda i: (i, U_QX // QX_WIDTH)),
            pl.BlockSpec((1, Q_LORA), lambda i: (0, 0)),
            pl.BlockSpec((Q_LORA, N_ATTN_HEADS * QK_PAD_DIM), lambda i: (0, 0)),
            tab, tab, tab,
        ],
        out_specs=pl.BlockSpec((1, N_ATTN_HEADS, tm, QK_PAD_DIM), lambda i: (i // nq, 0, i % nq, 0)),
        out_shape=jax.ShapeDtypeStruct((batch, N_ATTN_HEADS, seq, QK_PAD_DIM), BF16),
        compiler_params=_cparams("parallel"),
        name="q_projection",
    )(u, q_norm_w, w_q, cos, sina, sinb)


ATTN_TK = 512
ATTN_SLOTS = 2
KV_HEAD_GROUP = 4


def _kvproj_kernel(ckv_ref, kr_ref, nw_ref, wkt_ref, wv_ref, cos_ref, sina_ref, sinb_ref, kt_ref, v_ref):
    c = _rms(ckv_ref[...].astype(F32), nw_ref[...]).astype(BF16)
    kr = _apply_rope(kr_ref[...].astype(F32), cos_ref[...], sina_ref[...], sinb_ref[...])
    kr_t = kr.T.astype(BF16)
    nt = (((1,), (1,)), ((), ()))
    cols = KV_HEAD_GROUP * QK_NOPE_DIM
    for g in range(N_ATTN_HEADS // KV_HEAD_GROUP):
        kn_t = lax.dot_general(wkt_ref[g * cols:(g + 1) * cols, :], c, nt, preferred_element_type=F32).astype(BF16)
        v = jnp.dot(c, wv_ref[:, g * cols:(g + 1) * cols], preferred_element_type=F32).astype(BF16)
        for k in range(KV_HEAD_GROUP):
            h = g * KV_HEAD_GROUP + k
            kt_ref[0, h, 0, :QK_NOPE_DIM, :] = kn_t[k * QK_NOPE_DIM:(k + 1) * QK_NOPE_DIM, :]
            kt_ref[0, h, 0, QK_NOPE_DIM:, :] = kr_t
            v_ref[0, h] = v[:, k * V_HEAD_DIM:(k + 1) * V_HEAD_DIM]


def _kv_projection(u, kv_norm_w, w_kt, w_v, cos, sina, sinb, batch, seq):
    tk = ATTN_TK
    nk = seq // tk
    tab = pl.BlockSpec((tk, LANES), lambda i: (i, 0))
    return pl.pallas_call(
        _kvproj_kernel,
        grid=(batch * nk,),
        in_specs=[
            pl.BlockSpec((tk, KV_LORA), lambda i: (i, U_CKV // KV_LORA)),
            pl.BlockSpec((tk, LANES), lambda i: (i, (U_QX + Q_LORA) // LANES)),
            pl.BlockSpec((1, KV_LORA), lambda i: (0, 0)),
            pl.BlockSpec((N_ATTN_HEADS * QK_NOPE_DIM, KV_LORA), lambda i: (0, 0)),
            pl.BlockSpec((KV_LORA, N_ATTN_HEADS * V_HEAD_DIM), lambda i: (0, 0)),
            tab, tab, tab,
        ],
        out_specs=[
            pl.BlockSpec((1, N_ATTN_HEADS, 1, QK_PAD_DIM, tk), lambda i: (i // nk, 0, i % nk, 0, 0)),
            pl.BlockSpec((1, N_ATTN_HEADS, tk, V_HEAD_DIM), lambda i: (i // nk, 0, i % nk, 0)),
        ],
        out_shape=[
            jax.ShapeDtypeStruct((batch, N_ATTN_HEADS, nk, QK_PAD_DIM, tk), BF16),
            jax.ShapeDtypeStruct((batch, N_ATTN_HEADS, seq, V_HEAD_DIM), BF16),
        ],
        compiler_params=_cparams("parallel"),
        name="kv_projection",
    )(u, u, kv_norm_w, w_kt, w_v, cos, sina, sinb)


def _attn_kernel(q_ref, kt_ref, v_ref, z_ref, o_ref, s_ref, p_ref, a_ref, m_ref, l_ref, acc_ref, *, tk, nk, unroll):
    period = s_ref.shape[0]
    reps = tk // LANES
    m_ref[...] = jnp.full_like(m_ref, -jnp.inf)
    l_ref[...] = jnp.zeros_like(l_ref)
    acc_ref[...] = jnp.zeros_like(acc_ref)

    def scores(j, slot):
        s_ref[slot] = jnp.dot(q_ref[0, 0], kt_ref[0, 0, j], preferred_element_type=F32)

    def softmax(slot):
        s = s_ref[slot]
        m_prev = m_ref[...]
        m_new = jnp.maximum(m_prev, jnp.max(s, axis=1, keepdims=True))
        alpha = jnp.exp2(m_prev - m_new)
        p = jnp.exp2(s - jnp.concatenate([m_new] * reps, axis=1))
        l_part = p[:, :LANES]
        for r in range(1, reps):
            l_part = l_part + p[:, r * LANES:(r + 1) * LANES]
        l_ref[...] = alpha * l_ref[...] + l_part
        m_ref[...] = m_new
        a_ref[slot] = alpha
        p_ref[slot] = p.astype(BF16)

    def values(j, slot):
        start = pl.multiple_of(j * tk, tk)
        pv = jnp.dot(p_ref[slot], v_ref[0, 0, pl.ds(start, tk), :], preferred_element_type=F32)
        acc_ref[...] = a_ref[slot] * acc_ref[...] + pv

    def step(j, jm):
        traced = not isinstance(j, int)
        if traced or j + 1 < nk:
            scores(j + 1, (jm + 1) % period)
        softmax(jm % period)
        if traced or j >= 1:
            values(j - 1, (jm - 1) % period)

    assert unroll % period == 0, "buffer slots must be static in the loop body"
    scores(0, 0)
    step(0, 0)
    n_loop = max(nk - 2, 0) // unroll

    def group(i, carry):
        for u in range(unroll):
            step(1 + unroll * i + u, (1 + u) % period)
        return carry

    lax.fori_loop(0, n_loop, group, 0)
    for j in range(1 + unroll * n_loop, nk):
        step(j, j % period)
    values(nk - 1, (nk - 1) % period)
    l = jnp.sum(l_ref[...], axis=1, keepdims=True)
    o_ref[...] = (acc_ref[...] / l * _silu(z_ref[...].astype(F32))).astype(BF16)


def _attention(q, kt, v, u, batch, seq, tq, unroll):
    tk = ATTN_TK
    nk = seq // tk
    nq = seq // tq
    z_col0 = (U_GATES + D_MODEL) // V_HEAD_DIM
    stat = pltpu.VMEM((tq, LANES), F32)
    return pl.pallas_call(
        functools.partial(_attn_kernel, tk=tk, nk=nk, unroll=unroll),
        grid=(batch, N_ATTN_HEADS, nq),
        in_specs=[
            pl.BlockSpec((1, 1, tq, QK_PAD_DIM), lambda b, h, i: (b, h, i, 0)),
            pl.BlockSpec((1, 1, nk, QK_PAD_DIM, tk), lambda b, h, i: (b, h, 0, 0, 0)),
            pl.BlockSpec((1, 1, seq, V_HEAD_DIM), lambda b, h, i: (b, h, 0, 0)),
            pl.BlockSpec((tq, V_HEAD_DIM), lambda b, h, i: (b * nq + i, z_col0 + h)),
        ],
        out_specs=pl.BlockSpec((tq, V_HEAD_DIM), lambda b, h, i: (b * nq + i, h)),
        out_shape=jax.ShapeDtypeStruct((batch * seq, D_ATTN), BF16),
        scratch_shapes=[pltpu.VMEM((ATTN_SLOTS, tq, tk), F32), pltpu.VMEM((ATTN_SLOTS, tq, tk), BF16),
                        pltpu.VMEM((ATTN_SLOTS, tq, LANES), F32), stat, stat, pltpu.VMEM((tq, V_HEAD_DIM), F32)],
        compiler_params=_cparams("parallel", "parallel", "arbitrary"),
        name="attention",
    )(q, kt, v, u)


def _merge_kernel(y_ref, o_ref, gs_ref, ga_ref, wps_ref, wpa_ref, out_ref):
    a = jnp.dot(y_ref[...], wps_ref[...], preferred_element_type=F32)
    merged = _sigmoid(gs_ref[...].astype(F32)) * a
    b = jnp.dot(o_ref[...], wpa_ref[...], preferred_element_type=F32)
    merged = merged + _sigmoid(ga_ref[...].astype(F32)) * b
    out_ref[...] = merged.astype(BF16)


def _merge(y, o, gates, w_ps, w_pa, tm):
    n = y.shape[0]
    row = pl.BlockSpec((tm, D_MODEL), lambda i: (i, 0))
    wspec = pl.BlockSpec((D_MODEL, D_MODEL), lambda i: (0, 0))
    return pl.pallas_call(
        _merge_kernel,
        grid=(n // tm,),
        in_specs=[row, row,
                  pl.BlockSpec((tm, D_MODEL), lambda i: (i, 2)),
                  pl.BlockSpec((tm, D_MODEL), lambda i: (i, 3)),
                  wspec, wspec],
        out_specs=row,
        out_shape=jax.ShapeDtypeStruct((n, D_MODEL), BF16),
        compiler_params=_cparams("parallel"),
        name="gated_merge",
    )(y, o, gates, gates, w_ps, w_pa)


def _out_kernel(m_ref, x_ref, w_ref, nw_ref, out_ref):
    r = x_ref[...] + jnp.dot(m_ref[...], w_ref[...], preferred_element_type=F32)
    out_ref[...] = _rms(r, nw_ref[...])


def _out_projection(merged, x2, w_out, final_w, tm):
    n = x2.shape[0]
    row = pl.BlockSpec((tm, D_MODEL), lambda i: (i, 0))
    return pl.pallas_call(
        _out_kernel,
        grid=(n // tm,),
        in_specs=[row, row, pl.BlockSpec((D_MODEL, D_MODEL), lambda i: (0, 0)),
                  pl.BlockSpec((1, D_MODEL), lambda i: (0, 0))],
        out_specs=row,
        out_shape=jax.ShapeDtypeStruct((n, D_MODEL), F32),
        compiler_params=_cparams("parallel"),
        name="out_projection",
    )(merged, x2, w_out, final_w)


def _permute_w_in(w_in):
    sizes = (D_SSM, D_XBC, N_SSM_HEADS, N_SSM_HEADS, Q_LORA, KV_LORA, QK_ROPE_DIM, D_ATTN, D_MODEL, D_MODEL)
    offs = [0]
    for sz in sizes:
        offs.append(offs[-1] + sz)
    z_ssm, xbc, dt_f, dt_b, q_a, c_kv, k_r, z_attn, g_ssm, g_attn = (
        w_in[:, offs[k]:offs[k + 1]] for k in range(len(sizes)))
    pad64 = jnp.zeros((D_MODEL, LANES - QK_ROPE_DIM), w_in.dtype)
    cols = [z_ssm, z_attn, g_ssm, g_attn, q_a, k_r, pad64, dt_f, dt_b, pad64, xbc, c_kv]
    return jnp.concatenate(cols, axis=1).astype(BF16)


def _permute_w_uq(w_uq):
    w = w_uq.reshape(Q_LORA, N_ATTN_HEADS, QK_NOPE_DIM + QK_ROPE_DIM)
    pad = jnp.zeros((Q_LORA, N_ATTN_HEADS, QK_PAD_DIM - QK_NOPE_DIM - QK_ROPE_DIM), w.dtype)
    return jnp.concatenate([w, pad], axis=2).reshape(Q_LORA, N_ATTN_HEADS * QK_PAD_DIM).astype(BF16)


def _layer(x2, tabs, batch, seq, norm_w, w_in, conv_w, conv_b, a_log_fwd, a_log_bwd, dt_bias_fwd, dt_bias_bwd,
           d_skip, ssm_norm_w, q_norm_w, w_uq, kv_norm_w, w_ukv, w_proj_ssm, w_proj_attn, w_out):
    cos, sina, sinb = tabs
    row = lambda v: v.reshape(1, -1).astype(F32)
    pad_row = lambda f, b: jnp.concatenate([f, b, jnp.zeros((LANES - 2 * N_SSM_HEADS,), F32)]).reshape(1, LANES)

    u, dt = _in_projection(x2, row(norm_w), _permute_w_in(w_in), tm=1024)

    conv_w8 = jnp.concatenate([conv_w.astype(F32), jnp.zeros((4, D_XBC), F32)], axis=0)
    u3 = u.reshape(batch, seq, U_WIDTH)
    xc = _conv_silu(u3, conv_w8, row(conv_b), tm=512, tc=1024)
    y = _ssd(xc, dt.reshape(batch, seq, LANES), pad_row(dt_bias_fwd, dt_bias_bwd), pad_row(a_log_fwd, a_log_bwd),
             u3, row(jnp.repeat(d_skip, SSM_HEAD_DIM)), row(ssm_norm_w)).reshape(batch * seq, D_SSM)

    w_kv = w_ukv.reshape(KV_LORA, N_ATTN_HEADS, QK_NOPE_DIM + V_HEAD_DIM)
    w_kt = w_kv[:, :, :QK_NOPE_DIM].reshape(KV_LORA, N_ATTN_HEADS * QK_NOPE_DIM).T.astype(BF16)
    w_v = w_kv[:, :, QK_NOPE_DIM:].reshape(KV_LORA, N_ATTN_HEADS * V_HEAD_DIM).astype(BF16)
    q = _q_projection(u, row(q_norm_w), _permute_w_uq(w_uq), cos, sina, sinb, batch, seq, tm=512)
    kt, v = _kv_projection(u, row(kv_norm_w), w_kt, w_v, cos, sina, sinb, batch, seq)
    o = _attention(q, kt, v, u, batch, seq, tq=512, unroll=14)

    merged = _merge(y, o, u, w_proj_ssm.astype(BF16), w_proj_attn.astype(BF16), tm=256)
    return merged, w_out.astype(BF16)


def kernel(x, positions, norm_w, w_in, conv_w, conv_b, a_log_fwd, a_log_bwd, dt_bias_fwd, dt_bias_bwd, d_skip,
           ssm_norm_w, q_norm_w, w_uq, kv_norm_w, w_ukv, w_proj_ssm, w_proj_attn, w_out, final_norm_w):
    batch, seq, _ = x.shape
    depth = norm_w.shape[0]
    assert depth == 1, "the single output projection is fused with the final norm"
    inv = 1.0 / (ROPE_THETA ** (jnp.arange(0, QK_ROPE_DIM, 2, dtype=F32) / QK_ROPE_DIM))
    inv_row = jnp.concatenate([inv, inv, jnp.zeros((LANES - QK_ROPE_DIM,), F32)]).reshape(1, LANES)
    tabs = _rope_tables(positions.reshape(batch * seq, 1), inv_row, tm=1024)
    x2 = x.reshape(batch * seq, D_MODEL)
    merged, w_o = _layer(x2, tabs, batch, seq, norm_w[0], w_in[0], conv_w[0], conv_b[0], a_log_fwd[0], a_log_bwd[0],
                         dt_bias_fwd[0], dt_bias_bwd[0], d_skip[0], ssm_norm_w[0], q_norm_w[0], w_uq[0],
                         kv_norm_w[0], w_ukv[0], w_proj_ssm[0], w_proj_attn[0], w_out[0])
    out = _out_projection(merged, x2, w_o, final_norm_w.reshape(1, -1).astype(F32), tm=512)
    return out.reshape(batch, seq, D_MODEL)
```

```python
import functools
import math

import jax
import jax.numpy as jnp
from jax import lax
from jax.experimental import pallas as pl
from jax.experimental.pallas import tpu as pltpu

F32 = jnp.float32
BF16 = jnp.bfloat16

D_MODEL = 2048
D_SSM = 2048
SSM_HEAD_DIM = 64
N_SSM_HEADS = 32
SSM_GROUPS = 4
HEADS_PER_GROUP = N_SSM_HEADS // SSM_GROUPS
D_STATE = 128
CHUNK = 128
D_XBC = D_SSM + 2 * SSM_GROUPS * D_STATE
N_ATTN_HEADS = 16
QK_NOPE_DIM = 128
QK_ROPE_DIM = 64
V_HEAD_DIM = 128
D_ATTN = N_ATTN_HEADS * V_HEAD_DIM
Q_LORA = 768
KV_LORA = 512
ROPE_THETA = 10000.0
EPS = 1e-6

LANES = 128
QK_PAD_DIM = 2 * LANES
QX_WIDTH = 1024
GATES_WIDTH = 4 * D_MODEL
W_IN_PERM_WIDTH = GATES_WIDTH + D_XBC + KV_LORA + QX_WIDTH

VMEM_LIMIT = 56 * 1024 * 1024


def _cparams(*sem):
    return pltpu.CompilerParams(dimension_semantics=sem, vmem_limit_bytes=VMEM_LIMIT)


def _sigmoid(x):
    return 1.0 / (1.0 + jnp.exp(-x))


def _silu(x):
    return x * _sigmoid(x)


def _softplus(x):
    return jnp.maximum(x, 0.0) + jnp.log1p(jnp.exp(-jnp.abs(x)))


def _rms(x, w):
    ms = jnp.mean(x * x, axis=-1, keepdims=True)
    return x * lax.rsqrt(ms + EPS) * w


def _rope_kernel(pos_ref, inv_ref, cos_ref, sina_ref, sinb_ref):
    ang = pos_ref[...].astype(F32) * inv_ref[...]
    lane = lax.broadcasted_iota(jnp.int32, ang.shape, 1)
    c = jnp.cos(ang)
    s = jnp.sin(ang)
    cos_ref[...] = jnp.where(lane < 64, c, 0.0)
    sina_ref[...] = jnp.where(lane < 32, -s, 0.0)
    sinb_ref[...] = jnp.where((lane >= 32) & (lane < 64), s, 0.0)


def _rope_tables(pos_col, inv_row, tm):
    n = pos_col.shape[0]
    tab = jax.ShapeDtypeStruct((n, LANES), F32)
    spec = pl.BlockSpec((tm, LANES), lambda i: (i, 0))
    return pl.pallas_call(
        _rope_kernel,
        grid=(n // tm,),
        in_specs=[pl.BlockSpec((tm, 1), lambda i: (i, 0)), pl.BlockSpec((1, LANES), lambda i: (0, 0))],
        out_specs=[spec, spec, spec],
        out_shape=[tab, tab, tab],
        compiler_params=_cparams("parallel"),
        name="rope_tables",
    )(pos_col, inv_row)


def _apply_rope(rp, cos, sina, sinb):
    return rp * cos + pltpu.roll(rp, 96, 1) * sina + pltpu.roll(rp, 32, 1) * sinb


IN_TN = 512
U_GATES = 0
U_QX = U_GATES + GATES_WIDTH
U_XBC = U_QX + QX_WIDTH
U_CKV = U_XBC + D_XBC
U_WIDTH = U_CKV + KV_LORA
_J_DT = (U_QX + QX_WIDTH - LANES) // IN_TN
assert (U_QX + QX_WIDTH) % IN_TN == 0 and U_WIDTH == W_IN_PERM_WIDTH


def _inproj_kernel(x_ref, nw_ref, w_ref, u_ref, dt_ref, h_ref):
    j = pl.program_id(1)

    @pl.when(j == 0)
    def _():
        h_ref[...] = _rms(x_ref[...], nw_ref[...]).astype(BF16)

    acc = jnp.dot(h_ref[...], w_ref[...], preferred_element_type=F32)
    u_ref[...] = acc.astype(BF16)

    @pl.when(j == _J_DT)
    def _():
        dt_ref[...] = acc[:, IN_TN - LANES:]


def _in_projection(x2, norm_w, w_perm, tm):
    n = x2.shape[0]
    grid = (n // tm, U_WIDTH // IN_TN)
    return pl.pallas_call(
        _inproj_kernel,
        grid=grid,
        in_specs=[
            pl.BlockSpec((tm, D_MODEL), lambda i, j: (i, 0)),
            pl.BlockSpec((1, D_MODEL), lambda i, j: (0, 0)),
            pl.BlockSpec((D_MODEL, IN_TN), lambda i, j: (0, j)),
        ],
        out_specs=[
            pl.BlockSpec((tm, IN_TN), lambda i, j: (i, j)),
            pl.BlockSpec((tm, LANES), lambda i, j: (i, 0)),
        ],
        out_shape=[
            jax.ShapeDtypeStruct((n, U_WIDTH), BF16),
            jax.ShapeDtypeStruct((n, LANES), F32),
        ],
        scratch_shapes=[pltpu.VMEM((tm, D_MODEL), BF16)],
        compiler_params=_cparams("parallel", "arbitrary"),
        name="in_projection",
    )(x2, norm_w, w_perm)


CONV_HALO = 16
CONV_SUB = 128
CONV_PAD = CONV_SUB // 2
CONV_SHIFTS = (-2, -1, 1)


def _conv_shift_matrix():
    t = jnp.arange(CONV_SUB)[:, None]
    r = jnp.arange(2 * CONV_SUB)[None, :]
    return jnp.concatenate([(r == CONV_PAD + t + sh) for sh in CONV_SHIFTS], axis=0).astype(BF16)


def _conv_kernel(prev_ref, main_ref, next_ref, sel_ref, w_ref, b_ref, out_ref, ext_ref, *, tm):
    i = pl.program_id(1)
    last = pl.num_programs(1) - 1
    lo = CONV_PAD - CONV_HALO
    hi = CONV_PAD + tm + CONV_HALO
    zeros = jnp.zeros((lo, ext_ref.shape[1]), BF16)
    ext_ref[0:lo, :] = zeros
    ext_ref[hi:hi + lo, :] = zeros
    ext_ref[lo:CONV_PAD, :] = jnp.where(i > 0, prev_ref[0], jnp.zeros_like(prev_ref[0]))
    ext_ref[CONV_PAD:CONV_PAD + tm, :] = main_ref[0]
    ext_ref[CONV_PAD + tm:hi, :] = jnp.where(i < last, next_ref[0], jnp.zeros_like(next_ref[0]))
    sel = sel_ref[...]
    for sb in range(tm // CONV_SUB):
        r0 = sb * CONV_SUB
        shifted = jnp.dot(sel, ext_ref[r0:r0 + 2 * CONV_SUB, :], preferred_element_type=F32)
        acc = b_ref[...] + w_ref[2:3, :] * main_ref[0, r0:r0 + CONV_SUB, :].astype(F32)
        for j, tap in enumerate((0, 1, 3)):
            acc = acc + w_ref[tap:tap + 1, :] * shifted[j * CONV_SUB:(j + 1) * CONV_SUB, :]
        out_ref[0, r0:r0 + CONV_SUB, :] = _silu(acc).astype(BF16)


def _conv_silu(u3, conv_w8, conv_b, tm, tc):
    b, s, _ = u3.shape
    hb = tm // CONV_HALO
    n_halo = s // CONV_HALO
    c0 = U_XBC // tc
    sel = _conv_shift_matrix()
    return pl.pallas_call(
        functools.partial(_conv_kernel, tm=tm),
        grid=(b, s // tm, D_XBC // tc),
        in_specs=[
            pl.BlockSpec((1, CONV_HALO, tc), lambda bi, i, ci: (bi, jnp.maximum(i * hb - 1, 0), c0 + ci)),
            pl.BlockSpec((1, tm, tc), lambda bi, i, ci: (bi, i, c0 + ci)),
            pl.BlockSpec((1, CONV_HALO, tc), lambda bi, i, ci: (bi, jnp.minimum((i + 1) * hb, n_halo - 1), c0 + ci)),
            pl.BlockSpec(sel.shape, lambda bi, i, ci: (0, 0)),
            pl.BlockSpec((8, tc), lambda bi, i, ci: (0, ci)),
            pl.BlockSpec((1, tc), lambda bi, i, ci: (0, ci)),
        ],
        out_specs=pl.BlockSpec((1, tm, tc), lambda bi, i, ci: (bi, i, ci)),
        out_shape=jax.ShapeDtypeStruct((b, s, D_XBC), BF16),
        scratch_shapes=[pltpu.VMEM((tm + 2 * CONV_PAD, tc), BF16)],
        compiler_params=_cparams("parallel", "parallel", "parallel"),
        name="conv_silu",
    )(u3, u3, u3, sel, conv_w8, conv_b)


def _ssd_chunk(direction, xc_ref, dt_ref, bias_ref, alog_ref, state_ref, y_ref):
    fwd = direction == 0
    xc = xc_ref[...]
    dtv = _softplus(dt_ref[...] + bias_ref[...])
    da = dtv * (-jnp.exp(alog_ref[...]))
    row = lax.broadcasted_iota(jnp.int32, (CHUNK, CHUNK), 0)
    col = lax.broadcasted_iota(jnp.int32, (CHUNK, CHUNK), 1)
    keep = (row >= col) if fwd else (row <= col)
    a_cs = jnp.dot(keep.astype(F32), da, precision=lax.Precision.HIGHEST, preferred_element_type=F32)
    a_cs_t = a_cs.T
    dt_t = dtv.T
    end = CHUNK - 1 if fwd else 0
    lane = lax.broadcasted_iota(jnp.int32, (1, LANES), 1)
    first_half = lane < SSM_HEAD_DIM

    for g in range(SSM_GROUPS):
        b_g = xc[:, D_SSM + g * D_STATE:D_SSM + (g + 1) * D_STATE]
        c_g = xc[:, D_SSM + SSM_GROUPS * D_STATE + g * D_STATE:D_SSM + SSM_GROUPS * D_STATE + (g + 1) * D_STATE]
        cb = lax.dot_general(c_g, b_g, (((1,), (1,)), ((), ())), preferred_element_type=F32)
        c_f = c_g.astype(F32)
        b_t = b_g.astype(F32).T
        for pair in range(HEADS_PER_GROUP // 2):
            p_idx = g * (HEADS_PER_GROUP // 2) + pair
            x_pair = xc[:, p_idx * LANES:(p_idx + 1) * LANES]
            s_old = state_ref[:, p_idx * LANES:(p_idx + 1) * LANES]
            rhs = jnp.concatenate([x_pair, s_old.astype(BF16)], axis=0)
            ys, news, decs = [], [], []
            for k in range(2):
                cidx = direction * N_SSM_HEADS + 2 * p_idx + k
                col_a = a_cs[:, cidx:cidx + 1]
                row_a = a_cs_t[cidx:cidx + 1, :]
                row_dt = dt_t[cidx:cidx + 1, :]
                decay = jnp.where(keep, jnp.exp(col_a - row_a), 0.0)
                m_h = (cb * decay * row_dt).astype(BF16)
                c_e = (c_f * jnp.exp(col_a)).astype(BF16)
                lhs = jnp.concatenate([m_h, c_e], axis=1)
                ys.append(jnp.dot(lhs, rhs, preferred_element_type=F32))
                tot = row_a[:, end:end + 1]
                w_row = jnp.exp(tot - row_a) * row_dt
                b_w = (b_t * w_row).astype(BF16)
                news.append(jnp.dot(b_w, x_pair, preferred_element_type=F32))
                decs.append(jnp.exp(tot))
            y_ref[:, p_idx * LANES:(p_idx + 1) * LANES] = jnp.where(first_half, ys[0], ys[1])
            dec = jnp.where(first_half, decs[0], decs[1])
            state_ref[:, p_idx * LANES:(p_idx + 1) * LANES] = s_old * dec + jnp.where(first_half, news[0], news[1])


def _ssd_fwd_kernel(xc_ref, dt_ref, bias_ref, alog_ref, yf_ref, state_ref, y_ref):
    @pl.when(pl.program_id(0) == 0)
    def _():
        state_ref[...] = jnp.zeros_like(state_ref)

    for b in range(xc_ref.shape[0]):
        _ssd_chunk(0, xc_ref.at[b], dt_ref.at[b], bias_ref, alog_ref, state_ref.at[b], y_ref.at[b])
        yf_ref[b] = y_ref[b].astype(BF16)


def _ssd_bwd_kernel(xc_ref, dt_ref, bias_ref, alog_ref, yf_ref, z_ref, dskip_ref, nw_ref, out_ref, state_ref, y_ref):
    @pl.when(pl.program_id(0) == 0)
    def _():
        state_ref[...] = jnp.zeros_like(state_ref)

    gw = D_SSM // SSM_GROUPS
    for b in range(xc_ref.shape[0]):
        _ssd_chunk(1, xc_ref.at[b], dt_ref.at[b], bias_ref, alog_ref, state_ref.at[b], y_ref.at[b])
        for g in range(SSM_GROUPS):
            sl = slice(g * gw, (g + 1) * gw)
            xs = xc_ref[b, :, sl].astype(F32)
            y = yf_ref[b, :, sl].astype(F32) + y_ref[b, :, sl] + dskip_ref[:, sl] * xs
            y = y * _silu(z_ref[b, :, sl].astype(F32))
            out_ref[b, :, sl] = _rms(y, nw_ref[:, sl]).astype(BF16)


def _ssd(xc3, dt3, bias_row, alog_row, u3, d_skip_row, ssm_norm_w):
    batch, seq, _ = xc3.shape
    nc = seq // CHUNK
    row_spec = lambda width: pl.BlockSpec((batch, CHUNK, width), lambda c: (0, c, 0))
    rev_spec = lambda width: pl.BlockSpec((batch, CHUNK, width), lambda c: (0, nc - 1 - c, 0))
    const = lambda width: pl.BlockSpec((1, width), lambda c: (0, 0))
    scratch = [pltpu.VMEM((batch, D_STATE, D_SSM), F32), pltpu.VMEM((batch, CHUNK, D_SSM), F32)]
    out_shape = jax.ShapeDtypeStruct((batch, seq, D_SSM), BF16)
    y_fwd = pl.pallas_call(
        _ssd_fwd_kernel,
        grid=(nc,),
        in_specs=[row_spec(D_XBC), row_spec(LANES), const(LANES), const(LANES)],
        out_specs=row_spec(D_SSM),
        out_shape=out_shape,
        scratch_shapes=scratch,
        compiler_params=_cparams("arbitrary"),
        name="ssd_forward",
    )(xc3, dt3, bias_row, alog_row)
    return pl.pallas_call(
        _ssd_bwd_kernel,
        grid=(nc,),
        in_specs=[rev_spec(D_XBC), rev_spec(LANES), const(LANES), const(LANES), rev_spec(D_SSM),
                  rev_spec(D_SSM), const(D_SSM), const(D_SSM)],
        out_specs=rev_spec(D_SSM),
        out_shape=out_shape,
        scratch_shapes=scratch,
        compiler_params=_cparams("arbitrary"),
        name="ssd_backward",
    )(xc3, dt3, bias_row, alog_row, y_fwd, u3, d_skip_row, ssm_norm_w)


Q_SCALE = (1.0 / math.sqrt(QK_NOPE_DIM + QK_ROPE_DIM)) * math.log2(math.e)


def _qproj_kernel(qx_ref, nw_ref, w_ref, cos_ref, sina_ref, sinb_ref, q_ref):
    cq = _rms(qx_ref[:, :Q_LORA].astype(F32), nw_ref[...]).astype(BF16)
    cos, sina, sinb = cos_ref[...], sina_ref[...], sinb_ref[...]
    for h in range(N_ATTN_HEADS):
        a = jnp.dot(cq, w_ref[:, h * QK_PAD_DIM:(h + 1) * QK_PAD_DIM], preferred_element_type=F32)
        q_ref[0, h, :, :LANES] = (a[:, :LANES] * Q_SCALE).astype(BF16)
        q_ref[0, h, :, LANES:] = (_apply_rope(a[:, LANES:], cos, sina, sinb) * Q_SCALE).astype(BF16)


def _q_projection(u, q_norm_w, w_q, cos, sina, sinb, batch, seq, tm):
    nq = seq // tm
    tab = pl.BlockSpec((tm, LANES), lambda i: (i, 0))
    return pl.pallas_call(
        _qproj_kernel,
        grid=(batch * nq,),
        in_specs=[
            pl.BlockSpec((tm, QX_WIDTH), lambda i: (i, U_QX // QX_WIDTH)),
            pl.BlockSpec((1, Q_LORA), lambda i: (0, 0)),
            pl.BlockSpec((Q_LORA, N_ATTN_HEADS * QK_PAD_DIM), lambda i: (0, 0)),
            tab, tab, tab,
        ],
        out_specs=pl.BlockSpec((1, N_ATTN_HEADS, tm, QK_PAD_DIM), lambda i: (i // nq, 0, i % nq, 0)),
        out_shape=jax.ShapeDtypeStruct((batch, N_ATTN_HEADS, seq, QK_PAD_DIM), BF16),
        compiler_params=_cparams("parallel"),
        name="q_projection",
    )(u, q_norm_w, w_q, cos, sina, sinb)


ATTN_TK = 512
ATTN_SLOTS = 2
KV_HEAD_GROUP = 4


def _kvproj_kernel(ckv_ref, kr_ref, nw_ref, wkt_ref, wv_ref, cos_ref, sina_ref, sinb_ref, kt_ref, v_ref):
    c = _rms(ckv_ref[...].astype(F32), nw_ref[...]).astype(BF16)
    kr = _apply_rope(kr_ref[...].astype(F32), cos_ref[...], sina_ref[...], sinb_ref[...])
    kr_t = kr.T.astype(BF16)
    nt = (((1,), (1,)), ((), ()))
    cols = KV_HEAD_GROUP * QK_NOPE_DIM
    for g in range(N_ATTN_HEADS // KV_HEAD_GROUP):
        kn_t = lax.dot_general(wkt_ref[g * cols:(g + 1) * cols, :], c, nt, preferred_element_type=F32).astype(BF16)
        v = jnp.dot(c, wv_ref[:, g * cols:(g + 1) * cols], preferred_element_type=F32).astype(BF16)
        for k in range(KV_HEAD_GROUP):
            h = g * KV_HEAD_GROUP + k
            kt_ref[0, h, 0, :QK_NOPE_DIM, :] = kn_t[k * QK_NOPE_DIM:(k + 1) * QK_NOPE_DIM, :]
            kt_ref[0, h, 0, QK_NOPE_DIM:, :] = kr_t
            v_ref[0, h] = v[:, k * V_HEAD_DIM:(k + 1) * V_HEAD_DIM]


def _kv_projection(u, kv_norm_w, w_kt, w_v, cos, sina, sinb, batch, seq):
    tk = ATTN_TK
    nk = seq // tk
    tab = pl.BlockSpec((tk, LANES), lambda i: (i, 0))
    return pl.pallas_call(
        _kvproj_kernel,
        grid=(batch * nk,),
        in_specs=[
            pl.BlockSpec((tk, KV_LORA), lambda i: (i, U_CKV // KV_LORA)),
            pl.BlockSpec((tk, LANES), lambda i: (i, (U_QX + Q_LORA) // LANES)),
            pl.BlockSpec((1, KV_LORA), lambda i: (0, 0)),
            pl.BlockSpec((N_ATTN_HEADS * QK_NOPE_DIM, KV_LORA), lambda i: (0, 0)),
            pl.BlockSpec((KV_LORA, N_ATTN_HEADS * V_HEAD_DIM), lambda i: (0, 0)),
            tab, tab, tab,
        ],
        out_specs=[
            pl.BlockSpec((1, N_ATTN_HEADS, 1, QK_PAD_DIM, tk), lambda i: (i // nk, 0, i % nk, 0, 0)),
            pl.BlockSpec((1, N_ATTN_HEADS, tk, V_HEAD_DIM), lambda i: (i // nk, 0, i % nk, 0)),
        ],
        out_shape=[
            jax.ShapeDtypeStruct((batch, N_ATTN_HEADS, nk, QK_PAD_DIM, tk), BF16),
            jax.ShapeDtypeStruct((batch, N_ATTN_HEADS, seq, V_HEAD_DIM), BF16),
        ],
        compiler_params=_cparams("parallel"),
        name="kv_projection",
    )(u, u, kv_norm_w, w_kt, w_v, cos, sina, sinb)


def _attn_kernel(q_ref, kt_ref, v_ref, z_ref, o_ref, s_ref, p_ref, a_ref, m_ref, l_ref, acc_ref, *, tk, nk, unroll):
    period = s_ref.shape[0]
    reps = tk // LANES
    m_ref[...] = jnp.full_like(m_ref, -jnp.inf)
    l_ref[...] = jnp.zeros_like(l_ref)
    acc_ref[...] = jnp.zeros_like(acc_ref)

    def scores(j, slot):
        s_ref[slot] = jnp.dot(q_ref[0, 0], kt_ref[0, 0, j], preferred_element_type=F32)

    def softmax(slot):
        s = s_ref[slot]
        m_prev = m_ref[...]
        m_new = jnp.maximum(m_prev, jnp.max(s, axis=1, keepdims=True))
        alpha = jnp.exp2(m_prev - m_new)
        p = jnp.exp2(s - jnp.concatenate([m_new] * reps, axis=1))
        l_part = p[:, :LANES]
        for r in range(1, reps):
            l_part = l_part + p[:, r * LANES:(r + 1) * LANES]
        l_ref[...] = alpha * l_ref[...] + l_part
        m_ref[...] = m_new
        a_ref[slot] = alpha
        p_ref[slot] = p.astype(BF16)

    def values(j, slot):
        start = pl.multiple_of(j * tk, tk)
        pv = jnp.dot(p_ref[slot], v_ref[0, 0, pl.ds(start, tk), :], preferred_element_type=F32)
        acc_ref[...] = a_ref[slot] * acc_ref[...] + pv

    def step(j, jm):
        traced = not isinstance(j, int)
        if traced or j + 1 < nk:
            scores(j + 1, (jm + 1) % period)
        softmax(jm % period)
        if traced or j >= 1:
            values(j - 1, (jm - 1) % period)

    assert unroll % period == 0, "buffer slots must be static in the loop body"
    scores(0, 0)
    step(0, 0)
    n_loop = max(nk - 2, 0) // unroll

    def group(i, carry):
        for u in range(unroll):
            step(1 + unroll * i + u, (1 + u) % period)
        return carry

    lax.fori_loop(0, n_loop, group, 0)
    for j in range(1 + unroll * n_loop, nk):
        step(j, j % period)
    values(nk - 1, (nk - 1) % period)
    l = jnp.sum(l_ref[...], axis=1, keepdims=True)
    o_ref[...] = (acc_ref[...] / l * _silu(z_ref[...].astype(F32))).astype(BF16)


def _attention(q, kt, v, u, batch, seq, tq, unroll):
    tk = ATTN_TK
    nk = seq // tk
    nq = seq // tq
    z_col0 = (U_GATES + D_MODEL) // V_HEAD_DIM
    stat = pltpu.VMEM((tq, LANES), F32)
    return pl.pallas_call(
        functools.partial(_attn_kernel, tk=tk, nk=nk, unroll=unroll),
        grid=(batch, N_ATTN_HEADS, nq),
        in_specs=[
            pl.BlockSpec((1, 1, tq, QK_PAD_DIM), lambda b, h, i: (b, h, i, 0)),
            pl.BlockSpec((1, 1, nk, QK_PAD_DIM, tk), lambda b, h, i: (b, h, 0, 0, 0)),
            pl.BlockSpec((1, 1, seq, V_HEAD_DIM), lambda b, h, i: (b, h, 0, 0)),
            pl.BlockSpec((tq, V_HEAD_DIM), lambda b, h, i: (b * nq + i, z_col0 + h)),
        ],
        out_specs=pl.BlockSpec((tq, V_HEAD_DIM), lambda b, h, i: (b * nq + i, h)),
        out_shape=jax.ShapeDtypeStruct((batch * seq, D_ATTN), BF16),
        scratch_shapes=[pltpu.VMEM((ATTN_SLOTS, tq, tk), F32), pltpu.VMEM((ATTN_SLOTS, tq, tk), BF16),
                        pltpu.VMEM((ATTN_SLOTS, tq, LANES), F32), stat, stat, pltpu.VMEM((tq, V_HEAD_DIM), F32)],
        compiler_params=_cparams("parallel", "parallel", "arbitrary"),
        name="attention",
    )(q, kt, v, u)


def _merge_kernel(y_ref, o_ref, gs_ref, ga_ref, wps_ref, wpa_ref, out_ref):
    a = jnp.dot(y_ref[...], wps_ref[...], preferred_element_type=F32)
    merged = _sigmoid(gs_ref[...].astype(F32)) * a
    b = jnp.dot(o_ref[...], wpa_ref[...], preferred_element_type=F32)
    merged = merged + _sigmoid(ga_ref[...].astype(F32)) * b
    out_ref[...] = merged.astype(BF16)


def _merge(y, o, gates, w_ps, w_pa, tm):
    n = y.shape[0]
    row = pl.BlockSpec((tm, D_MODEL), lambda i: (i, 0))
    wspec = pl.BlockSpec((D_MODEL, D_MODEL), lambda i: (0, 0))
    return pl.pallas_call(
        _merge_kernel,
        grid=(n // tm,),
        in_specs=[row, row,
                  pl.BlockSpec((tm, D_MODEL), lambda i: (i, 2)),
                  pl.BlockSpec((tm, D_MODEL), lambda i: (i, 3)),
                  wspec, wspec],
        out_specs=row,
        out_shape=jax.ShapeDtypeStruct((n, D_MODEL), BF16),
        compiler_params=_cparams("parallel"),
        name="gated_merge",
    )(y, o, gates, gates, w_ps, w_pa)


def _out_kernel(m_ref, x_ref, w_ref, nw_ref, out_ref):
    r = x_ref[...] + jnp.dot(m_ref[...], w_ref[...], preferred_element_type=F32)
    out_ref[...] = _rms(r, nw_ref[...])


def _out_projection(merged, x2, w_out, final_w, tm):
    n = x2.shape[0]
    row = pl.BlockSpec((tm, D_MODEL), lambda i: (i, 0))
    return pl.pallas_call(
        _out_kernel,
        grid=(n // tm,),
        in_specs=[row, row, pl.BlockSpec((D_MODEL, D_MODEL), lambda i: (0, 0)),
                  pl.BlockSpec((1, D_MODEL), lambda i: (0, 0))],
        out_specs=row,
        out_shape=jax.ShapeDtypeStruct((n, D_MODEL), F32),
        compiler_params=_cparams("parallel"),
        name="out_projection",
    )(merged, x2, w_out, final_w)


def _permute_w_in(w_in):
    sizes = (D_SSM, D_XBC, N_SSM_HEADS, N_SSM_HEADS, Q_LORA, KV_LORA, QK_ROPE_DIM, D_ATTN, D_MODEL, D_MODEL)
    offs = [0]
    for sz in sizes:
        offs.append(offs[-1] + sz)
    z_ssm, xbc, dt_f, dt_b, q_a, c_kv, k_r, z_attn, g_ssm, g_attn = (
        w_in[:, offs[k]:offs[k + 1]] for k in range(len(sizes)))
    pad64 = jnp.zeros((D_MODEL, LANES - QK_ROPE_DIM), w_in.dtype)
    cols = [z_ssm, z_attn, g_ssm, g_attn, q_a, k_r, pad64, dt_f, dt_b, pad64, xbc, c_kv]
    return jnp.concatenate(cols, axis=1).astype(BF16)


def _permute_w_uq(w_uq):
    w = w_uq.reshape(Q_LORA, N_ATTN_HEADS, QK_NOPE_DIM + QK_ROPE_DIM)
    pad = jnp.zeros((Q_LORA, N_ATTN_HEADS, QK_PAD_DIM - QK_NOPE_DIM - QK_ROPE_DIM), w.dtype)
    return jnp.concatenate([w, pad], axis=2).reshape(Q_LORA, N_ATTN_HEADS * QK_PAD_DIM).astype(BF16)


def _layer(x2, tabs, batch, seq, norm_w, w_in, conv_w, conv_b, a_log_fwd, a_log_bwd, dt_bias_fwd, dt_bias_bwd,
           d_skip, ssm_norm_w, q_norm_w, w_uq, kv_norm_w, w_ukv, w_proj_ssm, w_proj_attn, w_out):
    cos, sina, sinb = tabs
    row = lambda v: v.reshape(1, -1).astype(F32)
    pad_row = lambda f, b: jnp.concatenate([f, b, jnp.zeros((LANES - 2 * N_SSM_HEADS,), F32)]).reshape(1, LANES)

    u, dt = _in_projection(x2, row(norm_w), _permute_w_in(w_in), tm=1024)

    conv_w8 = jnp.concatenate([conv_w.astype(F32), jnp.zeros((4, D_XBC), F32)], axis=0)
    u3 = u.reshape(batch, seq, U_WIDTH)
    xc = _conv_silu(u3, conv_w8, row(conv_b), tm=1024, tc=3072)
    y = _ssd(xc, dt.reshape(batch, seq, LANES), pad_row(dt_bias_fwd, dt_bias_bwd), pad_row(a_log_fwd, a_log_bwd),
             u3, row(jnp.repeat(d_skip, SSM_HEAD_DIM)), row(ssm_norm_w)).reshape(batch * seq, D_SSM)

    w_kv = w_ukv.reshape(KV_LORA, N_ATTN_HEADS, QK_NOPE_DIM + V_HEAD_DIM)
    w_kt = w_kv[:, :, :QK_NOPE_DIM].reshape(KV_LORA, N_ATTN_HEADS * QK_NOPE_DIM).T.astype(BF16)
    w_v = w_kv[:, :, QK_NOPE_DIM:].reshape(KV_LORA, N_ATTN_HEADS * V_HEAD_DIM).astype(BF16)
    q = _q_projection(u, row(q_norm_w), _permute_w_uq(w_uq), cos, sina, sinb, batch, seq, tm=512)
    kt, v = _kv_projection(u, row(kv_norm_w), w_kt, w_v, cos, sina, sinb, batch, seq)
    o = _attention(q, kt, v, u, batch, seq, tq=512, unroll=14)

    merged = _merge(y, o, u, w_proj_ssm.astype(BF16), w_proj_attn.astype(BF16), tm=256)
    return merged, w_out.astype(BF16)


def kernel(x, positions, norm_w, w_in, conv_w, conv_b, a_log_fwd, a_log_bwd, dt_bias_fwd, dt_bias_bwd, d_skip,
           ssm_norm_w, q_norm_w, w_uq, kv_norm_w, w_ukv, w_proj_ssm, w_proj_attn, w_out, final_norm_w):
    batch, seq, _ = x.shape
    depth = norm_w.shape[0]
    assert depth == 1, "the single output projection is fused with the final norm"
    inv = 1.0 / (ROPE_THETA ** (jnp.arange(0, QK_ROPE_DIM, 2, dtype=F32) / QK_ROPE_DIM))
    inv_row = jnp.concatenate([inv, inv, jnp.zeros((LANES - QK_ROPE_DIM,), F32)]).reshape(1, LANES)
    tabs = _rope_tables(positions.reshape(batch * seq, 1), inv_row, tm=1024)
    x2 = x.reshape(batch * seq, D_MODEL)
    merged, w_o = _layer(x2, tabs, batch, seq, norm_w[0], w_in[0], conv_w[0], conv_b[0], a_log_fwd[0], a_log_bwd[0],
                         dt_bias_fwd[0], dt_bias_bwd[0], d_skip[0], ssm_norm_w[0], q_norm_w[0], w_uq[0],
                         kv_norm_w[0], w_ukv[0], w_proj_ssm[0], w_proj_attn[0], w_out[0])
    out = _out_projection(merged, x2, w_o, final_norm_w.reshape(1, -1).astype(F32), tm=512)
    return out.reshape(batch, seq, D_MODEL)
```
